```python
import math
import jax, jax.numpy as jnp
from jax import lax
import numpy as np

D_MODEL = 2048
BATCH = 2
SEQ = 8192
DEPTH = 4
DEC_BATCH = 8
DEC_SEQ = 4096
PAST_LEN = 128

D_MIX = D_MODEL
ATTN_WIDTH = D_MIX // 2
LRU_WIDTH = D_MIX - ATTN_WIDTH
HEAD_DIM = 128
N_HEADS = ATTN_WIDTH // HEAD_DIM
N_KV_HEADS = 2
GROUP = N_HEADS // N_KV_HEADS
KV_WIDTH = N_KV_HEADS * HEAD_DIM
WINDOW = 128
BLOCK = WINDOW
N_BUCKETS = 32
MAX_DISTANCE = 128
LRU_BLOCKS = 8
LRU_BLOCK_DIM = LRU_WIDTH // LRU_BLOCKS
LRU_CONV = 4
LRU_C = 8.0
D_FF = 3 * D_MODEL
FFN_CONV = 3
EPS = 1e-6
IN_COLS = ATTN_WIDTH + 2 * KV_WIDTH + 2 * LRU_WIDTH

kernel_name = "hymba_window_gqa_rglru_convglu_adaln_encoder"


def rmsnorm(x, g):
    xf = x.astype(jnp.float32)
    y = xf * lax.rsqrt(jnp.mean(xf * xf, axis=-1, keepdims=True) + EPS)
    return (y * g.astype(jnp.float32)).astype(x.dtype)


def dwconv(x, w, b, pad_left):
    K = w.shape[0]
    S = x.shape[1]
    xp = jnp.pad(x, ((0, 0), (pad_left, K - 1 - pad_left), (0, 0)))
    out = xp[:, 0:S] * w[0] + b
    for k in range(1, K):
        out = out + xp[:, k:k + S] * w[k]
    return out


def t5_bucket(rel):
    nb = N_BUCKETS // 2
    max_exact = nb // 2
    n = jnp.abs(rel)
    nf = jnp.maximum(n, 1).astype(jnp.float32)
    large = max_exact + (jnp.log(nf / max_exact) / math.log(MAX_DISTANCE / max_exact)
                         * (nb - max_exact)).astype(jnp.int32)
    large = jnp.minimum(large, nb - 1)
    return jnp.where(rel > 0, nb, 0) + jnp.where(n < max_exact, n, large)


def band_blocks(t, nb):
    B, S = t.shape[0], t.shape[1]
    tp = jnp.pad(t, ((0, 0), (WINDOW, WINDOW), (0, 0), (0, 0)))
    parts = [tp[:, o * BLOCK:o * BLOCK + S].reshape(B, nb, BLOCK, N_KV_HEADS, HEAD_DIM) for o in range(3)]
    return jnp.concatenate(parts, axis=2)


def windowed_attention(q, k, v, sink, rel_bias):
    B, S = q.shape[0], q.shape[1]
    nb = S // BLOCK
    qi = jnp.arange(BLOCK)[:, None]
    kj = jnp.arange(3 * BLOCK)[None, :]
    rel = kj - BLOCK - qi
    bias = rel_bias[t5_bucket(rel)].astype(jnp.float32)
    bias = jnp.transpose(bias, (2, 0, 1)).reshape(N_KV_HEADS, GROUP, BLOCK, 3 * BLOCK)
    kpos = jnp.arange(nb)[:, None] * BLOCK - BLOCK + jnp.arange(3 * BLOCK)[None, :]
    valid = (jnp.abs(rel) <= WINDOW)[None] & ((kpos >= 0) & (kpos < S))[:, None, :]
    mask = valid[None, :, None, None]

    qb = q.reshape(B, nb, BLOCK, N_KV_HEADS, GROUP, HEAD_DIM).astype(jnp.float32)
    kb = band_blocks(k, nb).astype(jnp.float32)
    vb = band_blocks(v, nb).astype(jnp.float32)
    s = jnp.einsum('bnqhgd,bnkhd->bnhgqk', qb, kb) * (HEAD_DIM ** -0.5) + bias
    s = jnp.where(mask, s, -1e30)
    sk = sink.astype(jnp.float32).reshape(N_KV_HEADS, GROUP, 1, 1)
    m = jnp.maximum(jnp.max(s, axis=-1, keepdims=True), sk)
    p = jnp.exp(s - m)
    denom = jnp.sum(p, axis=-1, keepdims=True) + jnp.exp(sk - m)
    o = jnp.einsum('bnhgqk,bnkhd->bnqhgd', p / denom, vb)
    return o.reshape(B, S, ATTN_WIDTH).astype(q.dtype)


def linear_combine(e1, e2):
    a1, b1 = e1
    a2, b2 = e2
    return (a1 * a2, a2 * b1 + b2)


def rglru_branch(xr, yg, conv_w, conv_b, w_a, b_a, w_x, b_x, lam):
    B, S = xr.shape[0], xr.shape[1]
    xc = dwconv(xr, conv_w, conv_b, LRU_CONV // 2).astype(jnp.float32)
    xb = xc.reshape(B, S, LRU_BLOCKS, LRU_BLOCK_DIM)
    r = jax.nn.sigmoid(jnp.einsum('bsnc,rncj->rbsnj', xb, w_a.astype(jnp.float32)).reshape(2, B, S, LRU_WIDTH)
                       + b_a.astype(jnp.float32)[:, None, None])
    ig = jax.nn.sigmoid(jnp.einsum('bsnc,rncj->rbsnj', xb, w_x.astype(jnp.float32)).reshape(2, B, S, LRU_WIDTH)
                        + b_x.astype(jnp.float32)[:, None, None])
    log_a = -LRU_C * r * jax.nn.softplus(-lam.astype(jnp.float32))[:, None, None]
    a = jnp.exp(log_a)
    b = jnp.sqrt(-jnp.expm1(2.0 * log_a)) * ig * xc[None]
    _, h_f = lax.associative_scan(linear_combine, (a[0], b[0]), axis=1)
    _, h_b = lax.associative_scan(linear_combine, (a[1], b[1]), axis=1, reverse=True)
    return ((h_f + h_b) * jax.nn.gelu(yg.astype(jnp.float32))).astype(xr.dtype)


def setup_inputs(seed: int = 0) -> dict:
    key = jax.random.key(seed)
    ks = jax.random.split(key, 32)

    def nrm(k, shape, scale):
        return jax.random.normal(k, shape, jnp.float32) * scale

    def gain(k, shape):
        return 1.0 + 0.02 * jax.random.normal(k, shape, jnp.float32)

    u = jax.random.uniform(ks[17], (DEPTH, 2, LRU_WIDTH), jnp.float32, minval=0.9, maxval=0.999)
    a0 = u ** (1.0 / LRU_C)
    lam = jnp.log(a0) - jnp.log1p(-a0)
    return {
        "x_prompt": nrm(ks[0], (BATCH, SEQ, D_MODEL), 1.0),
        "x_sample": nrm(ks[1], (DEC_BATCH, DEC_SEQ, D_MODEL), 1.0),
        "c_prompt": nrm(ks[2], (BATCH, D_MODEL), 1.0),
        "c_sample": nrm(ks[3], (DEC_BATCH, D_MODEL), 1.0),
        "rel_bias": nrm(ks[4], (N_BUCKETS, N_HEADS), 0.5),
        "w_mod": nrm(ks[5], (DEPTH, D_MODEL, 6 * D_MODEL), 0.5 * D_MODEL ** -0.5),
        "b_mod": nrm(ks[6], (DEPTH, 6 * D_MODEL), 0.01),
        "norm1_g": gain(ks[7], (DEPTH, D_MODEL)),
        "norm2_g": gain(ks[8], (DEPTH, D_MODEL)),
        "w_in": nrm(ks[9], (DEPTH, D_MODEL, IN_COLS), D_MODEL ** -0.5),
        "q_norm_g": gain(ks[10], (DEPTH, HEAD_DIM)),
        "k_norm_g": gain(ks[11], (DEPTH, HEAD_DIM)),
        "attn_sink": nrm(ks[12], (DEPTH, N_HEADS), 1.0),
        "lru_conv_w": nrm(ks[13], (DEPTH, LRU_CONV, LRU_WIDTH), LRU_CONV ** -0.5),
        "lru_conv_b": nrm(ks[14], (DEPTH, LRU_WIDTH), 0.01),
        "lru_w_a": nrm(ks[15], (DEPTH, 2, LRU_BLOCKS, LRU_BLOCK_DIM, LRU_BLOCK_DIM), LRU_BLOCK_DIM ** -0.5),
        "lru_b_a": nrm(ks[16], (DEPTH, 2, LRU_WIDTH), 0.01),
        "lru_w_x": nrm(ks[18], (DEPTH, 2, LRU_BLOCKS, LRU_BLOCK_DIM, LRU_BLOCK_DIM), LRU_BLOCK_DIM ** -0.5),
        "lru_b_x": nrm(ks[19], (DEPTH, 2, LRU_WIDTH), 0.01),
        "lru_lambda": lam,
        "attn_out_g": gain(ks[20], (DEPTH, ATTN_WIDTH)),
        "lru_out_g": gain(ks[21], (DEPTH, LRU_WIDTH)),
        "w_out": nrm(ks[22], (DEPTH, D_MIX, D_MODEL), D_MIX ** -0.5),
        "ffn_w_gate": nrm(ks[23], (DEPTH, D_MODEL, D_FF), D_MODEL ** -0.5),
        "ffn_w_up": nrm(ks[24], (DEPTH, D_MODEL, D_FF), D_MODEL ** -0.5),
        "ffn_conv_w": nrm(ks[25], (DEPTH, FFN_CONV, D_FF), FFN_CONV ** -0.5),
        "ffn_conv_b": nrm(ks[26], (DEPTH, D_FF), 0.01),
        "ffn_w_down": nrm(ks[27], (DEPTH, D_FF, D_MODEL), D_FF ** -0.5),
    }


def reference(x_prompt, x_sample, c_prompt, c_sample, rel_bias, w_mod, b_mod, norm1_g, norm2_g, w_in,
              q_norm_g, k_norm_g, attn_sink, lru_conv_w, lru_conv_b, lru_w_a, lru_b_a, lru_w_x, lru_b_x,
              lru_lambda, attn_out_g, lru_out_g, w_out, ffn_w_gate, ffn_w_up, ffn_conv_w, ffn_conv_b,
              ffn_w_down):
    split_at = [ATTN_WIDTH, ATTN_WIDTH + KV_WIDTH, ATTN_WIDTH + 2 * KV_WIDTH,
                ATTN_WIDTH + 2 * KV_WIDTH + LRU_WIDTH]

    def mixer(h, l):
        B, S = h.shape[0], h.shape[1]
        z = h @ w_in[l]
        q, k, v, xr, yg = jnp.split(z, split_at, axis=-1)
        q = rmsnorm(q.reshape(B, S, N_HEADS, HEAD_DIM), q_norm_g[l])
        k = rmsnorm(k.reshape(B, S, N_KV_HEADS, HEAD_DIM), k_norm_g[l])
        v = v.reshape(B, S, N_KV_HEADS, HEAD_DIM)
        attn = windowed_attention(q, k, v, attn_sink[l], rel_bias)
        lru = rglru_branch(xr, yg, lru_conv_w[l], lru_conv_b[l], lru_w_a[l], lru_b_a[l],
                           lru_w_x[l], lru_b_x[l], lru_lambda[l])
        merged = jnp.concatenate([rmsnorm(attn, attn_out_g[l]), rmsnorm(lru, lru_out_g[l])], axis=-1)
        return merged @ w_out[l]

    def ffn(h, l):
        g = dwconv(h @ ffn_w_gate[l], ffn_conv_w[l], ffn_conv_b[l], FFN_CONV // 2)
        u = h @ ffn_w_up[l]
        return (jax.nn.gelu(g) * u) @ ffn_w_down[l]

    def run(x, c):
        cs = jax.nn.silu(c)
        for l in range(DEPTH):
            mod = (cs @ w_mod[l] + b_mod[l])[:, None, :]
            sh1, sc1, g1, sh2, sc2, g2 = jnp.split(mod, 6, axis=-1)
            h = rmsnorm(x, norm1_g[l]) * (1.0 + sc1) + sh1
            x = x + g1 * mixer(h, l)
            h = rmsnorm(x, norm2_g[l]) * (1.0 + sc2) + sh2
            x = x + g2 * ffn(h, l)
        return x

    y_prompt = run(x_prompt, c_prompt)
    y_sample = run(x_sample, c_sample)
    return (y_prompt, y_sample)
```

```python
import functools
import math

import jax
import jax.numpy as jnp
from jax import lax
from jax.experimental import pallas as pl
from jax.experimental.pallas import tpu as pltpu

F32 = jnp.float32
BF16 = jnp.bfloat16

EPS = 1e-6
HEAD_DIM = 128
N_HEADS = 8
N_KV_HEADS = 2
GROUP = N_HEADS // N_KV_HEADS
WINDOW = 128
N_BUCKETS = 32
LRU_BLOCKS = 8
LRU_BLOCK_DIM = 128
LRU_CONV = 4
LRU_C = 8.0
FFN_CONV = 3
NEG_INF = -1e30

SUBLANES = 8
LANES = 128
MIB = 1024 * 1024

TOKEN_TILE = 512
FF_CHUNK = 512
MOD_CHUNK = 1024
ATTN_Q_TILE = 1024
LRU_CHUNK = 512
HALO = SUBLANES


def _cparams(semantics, vmem_mib):
    return pltpu.CompilerParams(dimension_semantics=semantics, vmem_limit_bytes=vmem_mib * MIB)


def _dot(a, b):
    return jnp.dot(a, b, preferred_element_type=F32)


def _rms(x, gain):
    return x * lax.rsqrt(jnp.mean(x * x, axis=-1, keepdims=True) + EPS) * gain


def _gelu_tanh(x):
    return 0.5 * x * (1.0 + jnp.tanh(math.sqrt(2.0 / math.pi) * (x + 0.044715 * (x * x * x))))


def _sigmoid(x):
    return 1.0 / (1.0 + jnp.exp(-x))


def _one_minus_exp2(log_a, a):
    x = 2.0 * log_a
    series = -x * (1.0 + x * (1 / 2 + x * (1 / 6 + x * (1 / 24 + x * (1 / 120 + x * (1 / 720))))))
    return jnp.where(x > -0.1, series, 1.0 - a * a)


def _mod_kernel(c_ref, w_ref, b_ref, o_ref):
    c = c_ref[...]
    cs = (c * _sigmoid(c)).astype(BF16)
    o_ref[0] = _dot(cs, w_ref[0].astype(BF16)) + b_ref[0]


def _modulation(c_all, w_mod, b_mod):
    depth, d, n = w_mod.shape
    rows = c_all.shape[0]
    return pl.pallas_call(
        _mod_kernel,
        grid=(depth, n // MOD_CHUNK),
        in_specs=[
            pl.BlockSpec((rows, d), lambda l, j: (0, 0)),
            pl.BlockSpec((1, d, MOD_CHUNK), lambda l, j: (l, 0, j)),
            pl.BlockSpec((1, 1, MOD_CHUNK), lambda l, j: (l, 0, j)),
        ],
        out_specs=pl.BlockSpec((1, rows, MOD_CHUNK), lambda l, j: (l, 0, j)),
        out_shape=jax.ShapeDtypeStruct((depth, rows, n), F32),
        compiler_params=_cparams(("arbitrary", "arbitrary"), 40),
        name="adaln_modulation",
    )(c_all, w_mod, b_mod.reshape(depth, 1, n))


_BUCKET_THRESHOLDS = (12, 16, 23, 32, 46, 64, 91)


def _bias_kernel(rb_ref, o_ref):
    head = pl.program_id(0)
    qi = lax.broadcasted_iota(jnp.int32, (WINDOW, 3 * WINDOW), 0)
    kj = lax.broadcasted_iota(jnp.int32, (WINDOW, 3 * WINDOW), 1)
    rel = kj - WINDOW - qi
    n = jnp.abs(rel)
    half = N_BUCKETS // 2
    large = jnp.full_like(n, half // 2)
    for t in _BUCKET_THRESHOLDS:
        large = large + jnp.where(n >= t, 1, 0)
    bucket = jnp.where(rel > 0, half, 0) + jnp.where(n < half // 2, n, large)
    bias = jnp.zeros((WINDOW, 3 * WINDOW), F32)
    for b in range(N_BUCKETS):
        bias = jnp.where(bucket == b, rb_ref[b, head], bias)
    o_ref[0] = jnp.where(n <= WINDOW, bias, NEG_INF)


def _bias_table(rel_bias):
    out = pl.pallas_call(
        _bias_kernel,
        grid=(N_HEADS,),
        in_specs=[pl.BlockSpec(memory_space=pltpu.SMEM)],
        out_specs=pl.BlockSpec((1, WINDOW, 3 * WINDOW), lambda h: (h, 0, 0)),
        out_shape=jax.ShapeDtypeStruct((N_HEADS, WINDOW, 3 * WINDOW), F32),
        compiler_params=_cparams(("arbitrary",), 16),
        name="rel_bias_table",
    )(rel_bias)
    return out.reshape(N_KV_HEADS, GROUP * WINDOW, 3 * WINDOW)


def _in_kernel(x_ref, mod_ref, g_ref, w_ref, qg_ref, kg_ref, q_ref, k_ref, v_ref, xr_ref, yg_ref):
    mod = mod_ref[0]
    h = _rms(x_ref[...], g_ref[...]) * (1.0 + mod[1:2]) + mod[0:1]
    hb = h.astype(BF16)
    aw = q_ref.shape[1]
    kvw = k_ref.shape[1]
    lw = xr_ref.shape[1]
    nc = 4 * HEAD_DIM
    for c in range(aw // nc):
        z = _dot(hb, w_ref[:, c * nc:(c + 1) * nc])
        for i in range(nc // HEAD_DIM):
            zh = z[:, i * HEAD_DIM:(i + 1) * HEAD_DIM]
            q_ref[:, c * nc + i * HEAD_DIM:c * nc + (i + 1) * HEAD_DIM] = _rms(zh, qg_ref[...]).astype(BF16)
    z = _dot(hb, w_ref[:, aw:aw + 2 * kvw])
    for i in range(kvw // HEAD_DIM):
        zh = z[:, i * HEAD_DIM:(i + 1) * HEAD_DIM]
        k_ref[:, i * HEAD_DIM:(i + 1) * HEAD_DIM] = _rms(zh, kg_ref[...]).astype(BF16)
    v_ref[...] = z[:, kvw:].astype(BF16)
    base = aw + 2 * kvw
    for c in range(lw // nc):
        xr_ref[:, c * nc:(c + 1) * nc] = _dot(hb, w_ref[:, base + c * nc:base + (c + 1) * nc])
    base = base + lw
    for c in range(lw // nc):
        yg_ref[:, c * nc:(c + 1) * nc] = _dot(hb, w_ref[:, base + c * nc:base + (c + 1) * nc])


def _in_proj(x2, mod6, row0, seq, norm_g, w_in, q_g, k_g):
    t, d = x2.shape
    aw = N_HEADS * HEAD_DIM
    kvw = N_KV_HEADS * HEAD_DIM
    lw = LRU_BLOCKS * LRU_BLOCK_DIM
    tm = TOKEN_TILE
    tiles_per_seq = seq // tm
    tok = lambda w: pl.BlockSpec((tm, w), lambda i: (i, 0))
    const = lambda shape: pl.BlockSpec(shape, lambda i: (0,) * len(shape))
    return pl.pallas_call(
        _in_kernel,
        grid=(t // tm,),
        in_specs=[
            tok(d),
            pl.BlockSpec((1, 6, d), lambda i: (row0 + i // tiles_per_seq, 0, 0)),
            const((1, d)),
            const(w_in.shape),
            const((1, HEAD_DIM)),
            const((1, HEAD_DIM)),
        ],
        out_specs=[tok(aw), tok(kvw), tok(kvw), tok(lw), tok(lw)],
        out_shape=[
            jax.ShapeDtypeStruct((t, aw), BF16),
            jax.ShapeDtypeStruct((t, kvw), BF16),
            jax.ShapeDtypeStruct((t, kvw), BF16),
            jax.ShapeDtypeStruct((t, lw), F32),
            jax.ShapeDtypeStruct((t, lw), F32),
        ],
        compiler_params=_cparams(("parallel",), 52),
        name="in_proj",
    )(x2, mod6, norm_g.reshape(1, d), w_in, q_g.reshape(1, HEAD_DIM), k_g.reshape(1, HEAD_DIM))


def _attn_kernel(q_ref, k_ref, v_ref, bias_ref, sink_ref, o_ref, *, seq):
    nb = seq // WINDOW
    blocks = q_ref.shape[1] // WINDOW
    chunk = pl.program_id(2)
    scale = HEAD_DIM ** -0.5
    col = lax.broadcasted_iota(jnp.int32, (1, 3 * WINDOW), 1)

    def body(j, carry):
        n = chunk * blocks + j
        r0 = pl.multiple_of(j * WINDOW, WINDOW)
        qs = jnp.concatenate(
            [q_ref[0, pl.ds(r0, WINDOW), g * HEAD_DIM:(g + 1) * HEAD_DIM] for g in range(GROUP)], axis=0)
        starts = [pl.multiple_of(jnp.maximum(n - 1, 0) * WINDOW, WINDOW),
                  pl.multiple_of(n * WINDOW, WINDOW),
                  pl.multiple_of(jnp.minimum(n + 1, nb - 1) * WINDOW, WINDOW)]
        kw = jnp.concatenate([k_ref[0, pl.ds(s, WINDOW), :] for s in starts], axis=0)
        vw = jnp.concatenate([v_ref[0, pl.ds(s, WINDOW), :] for s in starts], axis=0)
        s = lax.dot_general(qs, kw, (((1,), (1,)), ((), ())), preferred_element_type=F32)
        s = s * scale + bias_ref[0]
        valid = ((col >= WINDOW) | (n > 0)) & ((col < 2 * WINDOW) | (n < nb - 1))
        s = jnp.where(valid, s, NEG_INF)
        sink = sink_ref[0]
        m = jnp.maximum(jnp.max(s, axis=-1, keepdims=True), sink)
        p = jnp.exp(s - m)
        denom = jnp.sum(p, axis=-1, keepdims=True) + jnp.exp(sink - m)
        o = _dot(p.astype(BF16), vw) / denom
        for g in range(GROUP):
            o_ref[0, pl.ds(r0, WINDOW), g * HEAD_DIM:(g + 1) * HEAD_DIM] = o[g * WINDOW:(g + 1) * WINDOW]
        return carry

    lax.fori_loop(0, blocks, body, 0)


def _attention(q, k, v, bias, sink_rows):
    b, seq, aw = q.shape
    tq = min(ATTN_Q_TILE, seq)
    gw = GROUP * HEAD_DIM
    return pl.pallas_call(
        functools.partial(_attn_kernel, seq=seq),
        grid=(b, N_KV_HEADS, seq // tq),
        in_specs=[
            pl.BlockSpec((1, tq, gw), lambda bi, h, c: (bi, c, h)),
            pl.BlockSpec((1, seq, HEAD_DIM), lambda bi, h, c: (bi, 0, h)),
            pl.BlockSpec((1, seq, HEAD_DIM), lambda bi, h, c: (bi, 0, h)),
            pl.BlockSpec((1, GROUP * WINDOW, 3 * WINDOW), lambda bi, h, c: (h, 0, 0)),
            pl.BlockSpec((1, GROUP * WINDOW, 1), lambda bi, h, c: (h, 0, 0)),
        ],
        out_specs=pl.BlockSpec((1, tq, gw), lambda bi, h, c: (bi, c, h)),
        out_shape=jax.ShapeDtypeStruct((b, seq, aw), F32),
        compiler_params=_cparams(("parallel", "parallel", "arbitrary"), 40),
        name="window_attention",
    )(q, k, v, bias, sink_rows)


def _lru_kernel(xr_ref, yg_ref, cw_ref, cb_ref, w_ref, gb_ref, lam_ref, o_ref,
                xp_ref, af_ref, bf_ref, ab_ref, bb_ref, *, seq, pitch):
    tc = min(LRU_CHUNK, seq)
    n_chunks = seq // tc
    pad = SUBLANES
    zeros8 = jnp.zeros((pad, LANES), F32)
    xp_ref[0:pad, :] = zeros8
    xp_ref[seq + pad:seq + 2 * pad, :] = zeros8
    tail = SUBLANES * pitch - seq
    for r in (af_ref, bf_ref, ab_ref, bb_ref):
        r[seq:seq + tail, :] = jnp.zeros((tail, LANES), F32)

    def copy_body(c, carry):
        t0 = pl.multiple_of(c * tc, tc)
        xp_ref[pl.ds(t0 + pad, tc), :] = xr_ref[0, pl.ds(t0, tc), :]
        return carry

    lax.fori_loop(0, n_chunks, copy_body, 0)

    log_decay = -LRU_C * jax.nn.softplus(-lam_ref[0])
    cw = cw_ref[...]
    cb = cb_ref[...]
    gb = gb_ref[0]
    bw = LRU_BLOCK_DIM

    def gate_body(c, carry):
        t0 = pl.multiple_of(c * tc, tc)
        xc = xp_ref[pl.ds(t0 + pad - 2, tc), :] * cw[0:1] + cb
        for kk in range(1, LRU_CONV):
            xc = xc + xp_ref[pl.ds(t0 + pad - 2 + kk, tc), :] * cw[kk:kk + 1]
        z = _dot(xc.astype(BF16), w_ref[0]) + gb
        for d, (a_ref, b_ref) in enumerate(((af_ref, bf_ref), (ab_ref, bb_ref))):
            r = _sigmoid(z[:, d * bw:(d + 1) * bw])
            ig = _sigmoid(z[:, (2 + d) * bw:(3 + d) * bw])
            log_a = r * log_decay[d:d + 1]
            a = jnp.exp(log_a)
            a_ref[pl.ds(t0, tc), :] = a
            b_ref[pl.ds(t0, tc), :] = jnp.sqrt(_one_minus_exp2(log_a, a)) * ig * xc
        return carry

    lax.fori_loop(0, n_chunks, gate_body, 0)

    def seg(i):
        return pl.ds(i, SUBLANES, stride=pitch)

    def scan_body(i, carry):
        hf, pf, hb, pb = carry
        a = af_ref[seg(i), :]
        hf = a * hf + bf_ref[seg(i), :]
        pf = a * pf
        bf_ref[seg(i), :] = hf
        af_ref[seg(i), :] = pf
        ib = pitch - 1 - i
        a = ab_ref[seg(ib), :]
        hb = a * hb + bb_ref[seg(ib), :]
        pb = a * pb
        bb_ref[seg(ib), :] = hb
        ab_ref[seg(ib), :] = pb
        return hf, pf, hb, pb

    zero = jnp.zeros((SUBLANES, LANES), F32)
    one = jnp.ones((SUBLANES, LANES), F32)
    hf, pf, hb, pb = lax.fori_loop(0, pitch, scan_body, (zero, one, zero, one))

    row = lax.broadcasted_iota(jnp.int32, (SUBLANES, LANES), 0)
    cf = zero
    c = jnp.zeros((1, LANES), F32)
    for s in range(1, SUBLANES):
        c = pf[s - 1:s] * c + hf[s - 1:s]
        cf = jnp.where(row == s, c, cf)
    cbk = zero
    c = jnp.zeros((1, LANES), F32)
    for s in range(SUBLANES - 2, -1, -1):
        c = pb[s + 1:s + 2] * c + hb[s + 1:s + 2]
        cbk = jnp.where(row == s, c, cbk)

    def fix_body(i, carry):
        bf_ref[seg(i), :] = (bf_ref[seg(i), :] + af_ref[seg(i), :] * cf
                             + bb_ref[seg(i), :] + ab_ref[seg(i), :] * cbk)
        return carry

    lax.fori_loop(0, pitch, fix_body, 0)

    def out_body(c, carry):
        t0 = pl.multiple_of(c * tc, tc)
        o_ref[0, pl.ds(t0, tc), :] = bf_ref[pl.ds(t0, tc), :] * _gelu_tanh(yg_ref[0, pl.ds(t0, tc), :])
        return carry

    lax.fori_loop(0, n_chunks, out_body, 0)


def _lru(xr, yg, conv_w, conv_b, w_cat, b_cat, lam):
    b, seq, lw = xr.shape
    pitch = seq // SUBLANES + 4
    bw = LRU_BLOCK_DIM
    seq_blk = pl.BlockSpec((1, seq, bw), lambda bi, n: (bi, 0, n))
    scan_rows = SUBLANES * pitch
    return pl.pallas_call(
        functools.partial(_lru_kernel, seq=seq, pitch=pitch),
        grid=(b, LRU_BLOCKS),
        in_specs=[
            seq_blk,
            seq_blk,
            pl.BlockSpec((LRU_CONV, bw), lambda bi, n: (0, n)),
            pl.BlockSpec((1, bw), lambda bi, n: (0, n)),
            pl.BlockSpec((1, bw, 4 * bw), lambda bi, n: (n, 0, 0)),
            pl.BlockSpec((1, 1, 4 * bw), lambda bi, n: (n, 0, 0)),
            pl.BlockSpec((1, 2, bw), lambda bi, n: (n, 0, 0)),
        ],
        out_specs=seq_blk,
        out_shape=jax.ShapeDtypeStruct((b, seq, lw), F32),
        scratch_shapes=[
            pltpu.VMEM((seq + 2 * SUBLANES, LANES), F32),
            pltpu.VMEM((scan_rows, LANES), F32),
            pltpu.VMEM((scan_rows, LANES), F32),
            pltpu.VMEM((scan_rows, LANES), F32),
            pltpu.VMEM((scan_rows, LANES), F32),
        ],
        compiler_params=_cparams(("parallel", "parallel"), 52),
        name="rglru",
    )(xr, yg, conv_w, conv_b.reshape(1, lw), w_cat, b_cat, lam)


def _out_kernel(x_ref, attn_ref, lru_ref, mod_ref, ag_ref, lg_ref, w_ref, o_ref):
    aw = attn_ref.shape[1]
    na = _rms(attn_ref[...], ag_ref[...]).astype(BF16)
    nl = _rms(lru_ref[...], lg_ref[...]).astype(BF16)
    y = _dot(na, w_ref[0:aw, :]) + _dot(nl, w_ref[aw:, :])
    o_ref[...] = x_ref[...] + mod_ref[0][2:3] * y


def _out_proj(x2, attn2, lru2, mod6, row0, seq, attn_g, lru_g, w_out):
    t, d = x2.shape
    aw = attn2.shape[1]
    lw = lru2.shape[1]
    tm = TOKEN_TILE
    tiles_per_seq = seq // tm
    tok = lambda w: pl.BlockSpec((tm, w), lambda i: (i, 0))
    const = lambda shape: pl.BlockSpec(shape, lambda i: (0,) * len(shape))
    return pl.pallas_call(
        _out_kernel,
        grid=(t // tm,),
        in_specs=[
            tok(d), tok(aw), tok(lw),
            pl.BlockSpec((1, 6, d), lambda i: (row0 + i // tiles_per_seq, 0, 0)),
            const((1, aw)), const((1, lw)), const(w_out.shape),
        ],
        out_specs=tok(d),
        out_shape=jax.ShapeDtypeStruct((t, d), F32),
        compiler_params=_cparams(("parallel",), 48),
        name="out_proj",
    )(x2, attn2, lru2, mod6, attn_g.reshape(1, aw), lru_g.reshape(1, lw), w_out)


def _ffn_kernel(x_ref, xp_ref, xn_ref, mod_ref, g_ref, wg_ref, wu_ref, cw_ref, cb_ref, wd_ref, o_ref,
                h_ref, *, tiles_per_seq):
    i = pl.program_id(0)
    j = pl.program_id(1)
    tm = x_ref.shape[0]

    @pl.when(j == 0)
    def _():
        mod = mod_ref[0]

        def norm_mod(x):
            return _rms(x, g_ref[...]) * (1.0 + mod[4:5]) + mod[3:4]

        first = (i % tiles_per_seq) == 0
        last = (i % tiles_per_seq) == tiles_per_seq - 1
        h_ref[0:HALO, :] = jnp.where(first, 0.0, norm_mod(xp_ref[...])).astype(BF16)
        h_ref[HALO:HALO + tm, :] = norm_mod(x_ref[...]).astype(BF16)
        h_ref[HALO + tm:, :] = jnp.where(last, 0.0, norm_mod(xn_ref[...])).astype(BF16)
        o_ref[...] = jnp.zeros_like(o_ref)

    ge = _dot(h_ref[...], wg_ref[...])
    u = _dot(h_ref[HALO:HALO + tm, :], wu_ref[...])
    cw = cw_ref[...]
    g = ge[HALO - 1:HALO - 1 + tm] * cw[0:1] + cb_ref[...]
    g = g + ge[HALO:HALO + tm] * cw[1:2]
    g = g + ge[HALO + 1:HALO + 1 + tm] * cw[2:3]
    act = (_gelu_tanh(g) * u).astype(BF16)
    o_ref[...] += _dot(act, wd_ref[...])

    @pl.when(j == pl.num_programs(1) - 1)
    def _():
        o_ref[...] = x_ref[...] + mod_ref[0][5:6] * o_ref[...]


def _ffn(x2, mod6, row0, seq, norm_g, w_gate, w_up, conv_w, conv_b, w_down):
    t, d = x2.shape
    f = w_gate.shape[1]
    tm = TOKEN_TILE
    fc = FF_CHUNK
    tiles_per_seq = seq // tm
    hb = tm // HALO
    n_halo_blocks = t // HALO
    return pl.pallas_call(
        functools.partial(_ffn_kernel, tiles_per_seq=tiles_per_seq),
        grid=(t // tm, f // fc),
        in_specs=[
            pl.BlockSpec((tm, d), lambda i, j: (i, 0)),
            pl.BlockSpec((HALO, d), lambda i, j: (jnp.maximum(i * hb - 1, 0), 0)),
            pl.BlockSpec((HALO, d), lambda i, j: (jnp.minimum((i + 1) * hb, n_halo_blocks - 1), 0)),
            pl.BlockSpec((1, 6, d), lambda i, j: (row0 + i // tiles_per_seq, 0, 0)),
            pl.BlockSpec((1, d), lambda i, j: (0, 0)),
            pl.BlockSpec((d, fc), lambda i, j: (0, j)),
            pl.BlockSpec((d, fc), lambda i, j: (0, j)),
            pl.BlockSpec((FFN_CONV, fc), lambda i, j: (0, j)),
            pl.BlockSpec((1, fc), lambda i, j: (0, j)),
            pl.BlockSpec((fc, d), lambda i, j: (j, 0)),
        ],
        out_specs=pl.BlockSpec((tm, d), lambda i, j: (i, 0)),
        out_shape=jax.ShapeDtypeStruct((t, d), F32),
        scratch_shapes=[pltpu.VMEM((tm + 2 * HALO, d), BF16)],
        compiler_params=_cparams(("parallel", "arbitrary"), 48),
        name="convglu_ffn",
    )(x2, x2, x2, mod6, norm_g.reshape(1, d), w_gate, w_up, conv_w, conv_b.reshape(1, f), w_down)


def kernel(x_prompt, x_sample, c_prompt, c_sample, rel_bias, w_mod, b_mod, norm1_g, norm2_g, w_in, q_norm_g, k_norm_g, attn_sink, lru_conv_w, lru_conv_b, lru_w_a, lru_b_a, lru_w_x, lru_b_x, lru_lambda, attn_out_g, lru_out_g, w_out, ffn_w_gate, ffn_w_up, ffn_conv_w, ffn_conv_b, ffn_w_down):
    depth, d, _ = w_in.shape
    n_prompt = c_prompt.shape[0]
    n_req = n_prompt + c_sample.shape[0]
    rows = -(-n_req // SUBLANES) * SUBLANES
    c_all = jnp.concatenate([c_prompt, c_sample, jnp.zeros((rows - n_req, d), F32)], axis=0)
    mod = _modulation(c_all, w_mod, b_mod).reshape(depth, rows, 6, d)
    bias = _bias_table(rel_bias)

    w_in_b = w_in.astype(BF16)
    w_out_b = w_out.astype(BF16)
    w_gate_b = ffn_w_gate.astype(BF16)
    w_up_b = ffn_w_up.astype(BF16)
    w_down_b = ffn_w_down.astype(BF16)
    w_cat = jnp.concatenate([lru_w_a[:, 0], lru_w_a[:, 1], lru_w_x[:, 0], lru_w_x[:, 1]], axis=-1).astype(BF16)
    bw = LRU_BLOCK_DIM
    b_cat = jnp.concatenate(
        [lru_b_a.reshape(depth, 2, LRU_BLOCKS, bw), lru_b_x.reshape(depth, 2, LRU_BLOCKS, bw)], axis=1)
    b_cat = jnp.transpose(b_cat, (0, 2, 1, 3)).reshape(depth, LRU_BLOCKS, 1, 4 * bw)
    lam = jnp.transpose(lru_lambda.reshape(depth, 2, LRU_BLOCKS, bw), (0, 2, 1, 3))
    sink_rows = jnp.repeat(attn_sink.reshape(depth, N_KV_HEADS, GROUP), WINDOW, axis=-1)[..., None]

    def run(x, row0):
        b, seq, _ = x.shape
        x2 = x.reshape(b * seq, d)
        for l in range(depth):
            q, k, v, xr, yg = _in_proj(x2, mod[l], row0, seq, norm1_g[l], w_in_b[l], q_norm_g[l], k_norm_g[l])
            sh = lambda a: a.reshape(b, seq, a.shape[-1])
            attn = _attention(sh(q), sh(k), sh(v), bias, sink_rows[l])
            lru = _lru(sh(xr), sh(yg), lru_conv_w[l], lru_conv_b[l], w_cat[l], b_cat[l], lam[l])
            x2 = _out_proj(x2, attn.reshape(b * seq, -1), lru.reshape(b * seq, -1), mod[l], row0, seq,
                           attn_out_g[l], lru_out_g[l], w_out_b[l])
            x2 = _ffn(x2, mod[l], row0, seq, norm2_g[l], w_gate_b[l], w_up_b[l], ffn_conv_w[l], ffn_conv_b[l],
                      w_down_b[l])
        return x2.reshape(b, seq, d)

    return (run(x_prompt, 0), run(x_sample, n_prompt))
```

```python
import functools
import math

import jax
import jax.numpy as jnp
from jax import lax
from jax.experimental import pallas as pl
from jax.experimental.pallas import tpu as pltpu

F32 = jnp.float32
BF16 = jnp.bfloat16

EPS = 1e-6
HEAD_DIM = 128
N_HEADS = 8
N_KV_HEADS = 2
GROUP = N_HEADS // N_KV_HEADS
WINDOW = 128
N_BUCKETS = 32
LRU_BLOCKS = 8
LRU_BLOCK_DIM = 128
LRU_CONV = 4
LRU_C = 8.0
FFN_CONV = 3
NEG_INF = -1e30
LOG2E = math.log2(math.e)
LN2 = math.log(2.0)
SERIES_LIMIT = 0.02
F32_TINY = float(jnp.finfo(jnp.float32).tiny)

SUBLANES = 8
LANES = 128
MIB = 1024 * 1024

TOKEN_TILE = 512
FF_CHUNK = 512
MOD_CHUNK = 1024
ATTN_Q_TILE = 1024
LRU_CHUNK = 512
LRU_SCAN_UNROLL = 12
LRU_SCAN_BATCH = 6
LRU_LOOKAHEAD = 3
ATTN_UNROLL = 4
HALO = SUBLANES


def _cparams(semantics, vmem_mib):
    return pltpu.CompilerParams(dimension_semantics=semantics, vmem_limit_bytes=vmem_mib * MIB)


def _dot(a, b):
    return jnp.dot(a, b, preferred_element_type=F32)


def _rms(x, gain):
    return x * lax.rsqrt(jnp.mean(x * x, axis=-1, keepdims=True) + EPS) * gain


def _gelu_tanh(x):
    return 0.5 * x * (1.0 + jnp.tanh(math.sqrt(2.0 / math.pi) * (x + 0.044715 * (x * x * x))))


def _sigmoid(x):
    return 1.0 / (1.0 + jnp.exp(-x))


def _one_minus_exp(x, exp_x):
    series = x * (-1.0 + x * (-1 / 2 + x * (-1 / 6 + x * (-1 / 24))))
    return jnp.where(x > -SERIES_LIMIT, series, 1.0 - exp_x)


def _mod_kernel(c_ref, w_ref, b_ref, o_ref):
    c = c_ref[...]
    cs = (c * _sigmoid(c)).astype(BF16)
    o_ref[0] = _dot(cs, w_ref[0].astype(BF16)) + b_ref[0]


def _modulation(c_all, w_mod, b_mod):
    depth, d, n = w_mod.shape
    rows = c_all.shape[0]
    return pl.pallas_call(
        _mod_kernel,
        grid=(depth, n // MOD_CHUNK),
        in_specs=[
            pl.BlockSpec((rows, d), lambda l, j: (0, 0)),
            pl.BlockSpec((1, d, MOD_CHUNK), lambda l, j: (l, 0, j)),
            pl.BlockSpec((1, 1, MOD_CHUNK), lambda l, j: (l, 0, j)),
        ],
        out_specs=pl.BlockSpec((1, rows, MOD_CHUNK), lambda l, j: (l, 0, j)),
        out_shape=jax.ShapeDtypeStruct((depth, rows, n), F32),
        compiler_params=_cparams(("arbitrary", "arbitrary"), 40),
        name="adaln_modulation",
    )(c_all, w_mod, b_mod.reshape(depth, 1, n))


_BUCKET_THRESHOLDS = (12, 16, 23, 32, 46, 64, 91)


def _bias_kernel(rb_ref, o_ref):
    head = pl.program_id(0)
    qi = lax.broadcasted_iota(jnp.int32, (WINDOW, 3 * WINDOW), 0)
    kj = lax.broadcasted_iota(jnp.int32, (WINDOW, 3 * WINDOW), 1)
    rel = kj - WINDOW - qi
    n = jnp.abs(rel)
    half = N_BUCKETS // 2
    large = jnp.full_like(n, half // 2)
    for t in _BUCKET_THRESHOLDS:
        large = large + jnp.where(n >= t, 1, 0)
    bucket = jnp.where(rel > 0, half, 0) + jnp.where(n < half // 2, n, large)
    bias = jnp.zeros((WINDOW, 3 * WINDOW), F32)
    for b in range(N_BUCKETS):
        bias = jnp.where(bucket == b, rb_ref[b, head], bias)
    o_ref[0] = jnp.where(n <= WINDOW, bias * LOG2E, NEG_INF)


def _bias_table(rel_bias):
    out = pl.pallas_call(
        _bias_kernel,
        grid=(N_HEADS,),
        in_specs=[pl.BlockSpec(memory_space=pltpu.SMEM)],
        out_specs=pl.BlockSpec((1, WINDOW, 3 * WINDOW), lambda h: (h, 0, 0)),
        out_shape=jax.ShapeDtypeStruct((N_HEADS, WINDOW, 3 * WINDOW), F32),
        compiler_params=_cparams(("arbitrary",), 16),
        name="rel_bias_table",
    )(rel_bias)
    return out.reshape(N_KV_HEADS, GROUP * WINDOW, 3 * WINDOW)


def _in_kernel(x_ref, mod_ref, g_ref, w_ref, qg_ref, kg_ref, q_ref, k_ref, v_ref, xr_ref, yg_ref):
    mod = mod_ref[0]
    h = _rms(x_ref[...], g_ref[...]) * (1.0 + mod[1:2]) + mod[0:1]
    hb = h.astype(BF16)
    aw = q_ref.shape[1]
    kvw = k_ref.shape[1]
    lw = xr_ref.shape[1]
    nc = 4 * HEAD_DIM
    for c in range(aw // nc):
        z = _dot(hb, w_ref[:, c * nc:(c + 1) * nc])
        for i in range(nc // HEAD_DIM):
            zh = z[:, i * HEAD_DIM:(i + 1) * HEAD_DIM]
            q_ref[:, c * nc + i * HEAD_DIM:c * nc + (i + 1) * HEAD_DIM] = _rms(zh, qg_ref[...]).astype(BF16)
    z = _dot(hb, w_ref[:, aw:aw + 2 * kvw])
    for i in range(kvw // HEAD_DIM):
        zh = z[:, i * HEAD_DIM:(i + 1) * HEAD_DIM]
        k_ref[:, i * HEAD_DIM:(i + 1) * HEAD_DIM] = _rms(zh, kg_ref[...]).astype(BF16)
    v_ref[...] = z[:, kvw:].astype(BF16)
    base = aw + 2 * kvw
    for c in range(lw // nc):
        xr_ref[:, c * nc:(c + 1) * nc] = _dot(hb, w_ref[:, base + c * nc:base + (c + 1) * nc])
    base = base + lw
    for c in range(lw // nc):
        yg_ref[:, c * nc:(c + 1) * nc] = _dot(hb, w_ref[:, base + c * nc:base + (c + 1) * nc])


def _in_proj(x2, mod6, row0, seq, norm_g, w_in, layer, q_g, k_g):
    t, d = x2.shape
    aw = N_HEADS * HEAD_DIM
    kvw = N_KV_HEADS * HEAD_DIM
    lw = LRU_BLOCKS * LRU_BLOCK_DIM
    tm = TOKEN_TILE
    tiles_per_seq = seq // tm
    tok = lambda w: pl.BlockSpec((tm, w), lambda i: (i, 0))
    const = lambda shape: pl.BlockSpec(shape, lambda i: (0,) * len(shape))
    return pl.pallas_call(
        _in_kernel,
        grid=(t // tm,),
        in_specs=[
            tok(d),
            pl.BlockSpec((1, 6, d), lambda i: (row0 + i // tiles_per_seq, 0, 0)),
            const((1, d)),
            pl.BlockSpec((None,) + w_in.shape[1:], lambda i: (layer, 0, 0)),
            const((1, HEAD_DIM)),
            const((1, HEAD_DIM)),
        ],
        out_specs=[tok(aw), tok(kvw), tok(kvw), tok(lw), tok(lw)],
        out_shape=[
            jax.ShapeDtypeStruct((t, aw), BF16),
            jax.ShapeDtypeStruct((t, kvw), BF16),
            jax.ShapeDtypeStruct((t, kvw), BF16),
            jax.ShapeDtypeStruct((t, lw), F32),
            jax.ShapeDtypeStruct((t, lw), F32),
        ],
        compiler_params=_cparams(("parallel",), 52),
        name="in_proj",
    )(x2, mod6, norm_g.reshape(1, d), w_in, q_g.reshape(1, HEAD_DIM), k_g.reshape(1, HEAD_DIM))


def _attn_kernel(q_ref, k_ref, v_ref, bias_ref, sink_ref, o_ref, *, seq):
    nb = seq // WINDOW
    blocks = q_ref.shape[1] // WINDOW
    chunk = pl.program_id(2)
    scale = HEAD_DIM ** -0.5 * LOG2E
    sink = sink_ref[0] * LOG2E
    col = lax.broadcasted_iota(jnp.int32, (1, 3 * WINDOW), 1)
    ones = jnp.ones((3 * WINDOW, HEAD_DIM), BF16)
    rows = GROUP * WINDOW

    def scores(j):
        n = chunk * blocks + j
        r0 = pl.multiple_of(j * WINDOW, WINDOW)
        qs = jnp.concatenate(
            [q_ref[0, pl.ds(r0, WINDOW), g * HEAD_DIM:(g + 1) * HEAD_DIM] for g in range(GROUP)], axis=0)
        starts = [pl.multiple_of(jnp.maximum(n - 1, 0) * WINDOW, WINDOW),
                  pl.multiple_of(n * WINDOW, WINDOW),
                  pl.multiple_of(jnp.minimum(n + 1, nb - 1) * WINDOW, WINDOW)]
        kw = jnp.concatenate([k_ref[0, pl.ds(s, WINDOW), :] for s in starts], axis=0)
        vw = jnp.concatenate(
            [jnp.concatenate([v_ref[0, pl.ds(s, WINDOW), :] for s in starts], axis=0), ones], axis=1)
        s = lax.dot_general(qs, kw, (((1,), (1,)), ((), ())), preferred_element_type=F32)
        return n, r0, s, vw

    def softmax(n, r0, s, vw):
        s = s * scale + bias_ref[0]
        valid = ((col >= WINDOW) | (n > 0)) & ((col < 2 * WINDOW) | (n < nb - 1))
        s = jnp.where(valid, s, NEG_INF)
        m = jnp.maximum(jnp.broadcast_to(jnp.max(s, axis=-1, keepdims=True), (rows, HEAD_DIM)), sink)
        p = jnp.exp2(s - jnp.concatenate([m, m, m], axis=1)).astype(BF16)
        return r0, p, jnp.exp2(sink - m), vw

    def finish(r0, p, sink_term, vw):
        ov = _dot(p, vw)
        o = ov[:, :HEAD_DIM] / (ov[:, HEAD_DIM:] + sink_term)
        for g in range(GROUP):
            o_ref[0, pl.ds(r0, WINDOW), g * HEAD_DIM:(g + 1) * HEAD_DIM] = o[g * WINDOW:(g + 1) * WINDOW]

    def body(jj, carry):
        stage = [scores(jj * ATTN_UNROLL + u) for u in range(ATTN_UNROLL)]
        stage = [softmax(*st) for st in stage]
        for st in stage:
            finish(*st)
        return carry

    lax.fori_loop(0, blocks // ATTN_UNROLL, body, 0)


def _attention(q, k, v, bias, sink_rows):
    b, seq, aw = q.shape
    tq = min(ATTN_Q_TILE, seq)
    gw = GROUP * HEAD_DIM
    return pl.pallas_call(
        functools.partial(_attn_kernel, seq=seq),
        grid=(b, N_KV_HEADS, seq // tq),
        in_specs=[
            pl.BlockSpec((1, tq, gw), lambda bi, h, c: (bi, c, h)),
            pl.BlockSpec((1, seq, HEAD_DIM), lambda bi, h, c: (bi, 0, h)),
            pl.BlockSpec((1, seq, HEAD_DIM), lambda bi, h, c: (bi, 0, h)),
            pl.BlockSpec((1, GROUP * WINDOW, 3 * WINDOW), lambda bi, h, c: (h, 0, 0)),
            pl.BlockSpec((1, GROUP * WINDOW, HEAD_DIM), lambda bi, h, c: (h, 0, 0)),
        ],
        out_specs=pl.BlockSpec((1, tq, gw), lambda bi, h, c: (bi, c, h)),
        out_shape=jax.ShapeDtypeStruct((b, seq, aw), F32),
        compiler_params=_cparams(("parallel", "parallel", "arbitrary"), 40),
        name="window_attention",
    )(q, k, v, bias, sink_rows)


def _lru_kernel(xr_ref, yg_ref, cw_ref, cb_ref, w_ref, gb_ref, lam_ref, o_ref,
                xp_ref, af_ref, bf_ref, ab_ref, bb_ref, carry_ref, *, seq, pitch):
    tc = min(LRU_CHUNK, seq)
    n_chunks = seq // tc
    pad = SUBLANES
    zeros8 = jnp.zeros((pad, LANES), F32)
    xp_ref[0:pad, :] = zeros8
    xp_ref[seq + pad:seq + 2 * pad, :] = zeros8
    tail = SUBLANES * pitch - seq
    for r in (af_ref, bf_ref, ab_ref, bb_ref):
        r[seq:seq + tail, :] = jnp.zeros((tail, LANES), F32)

    def copy_body(c, carry):
        t0 = pl.multiple_of(c * tc, tc)
        xp_ref[pl.ds(t0 + pad, tc), :] = xr_ref[0, pl.ds(t0, tc), :]
        return carry

    lax.fori_loop(0, n_chunks, copy_body, 0)

    half_log2_decay = (-0.5 * LRU_C * LOG2E) * jax.nn.softplus(-lam_ref[0])
    cw = cw_ref[...]
    cb = cb_ref[...]
    gb = gb_ref[0]
    bw = LRU_BLOCK_DIM

    def gate_body(c, carry):
        t0 = pl.multiple_of(c * tc, tc)
        xc = xp_ref[pl.ds(t0 + pad - 2, tc), :] * cw[0:1] + cb
        for kk in range(1, LRU_CONV):
            xc = xc + xp_ref[pl.ds(t0 + pad - 2 + kk, tc), :] * cw[kk:kk + 1]
        t = jnp.tanh(_dot(xc.astype(BF16), w_ref[0]) + gb)
        xh = 0.5 * xc
        for d, (a_ref, b_ref) in enumerate(((af_ref, bf_ref), (ab_ref, bb_ref))):
            log2_a = (1.0 + t[:, d * bw:(d + 1) * bw]) * half_log2_decay[d:d + 1]
            a = jnp.exp2(log2_a)
            a_ref[pl.ds(t0, tc), :] = a
            y = _one_minus_exp(log2_a * (2.0 * LN2), a * a)
            root = y * lax.rsqrt(jnp.maximum(y, F32_TINY))
            b_ref[pl.ds(t0, tc), :] = root * (1.0 + t[:, (2 + d) * bw:(3 + d) * bw]) * xh
        return carry

    lax.fori_loop(0, n_chunks, gate_body, 0)

    def seg(i):
        return pl.ds(i, SUBLANES, stride=pitch)

    def scan_steps(a_ref, b_ref, steps, h, p):
        ab = [(a_ref[seg(i), :], b_ref[seg(i), :]) for i in steps]
        for g0 in range(0, len(steps), LRU_LOOKAHEAD):
            a_run = b_run = None
            for i, (a, b) in zip(steps[g0:g0 + LRU_LOOKAHEAD], ab[g0:g0 + LRU_LOOKAHEAD]):
                a_run, b_run = (a, b) if a_run is None else (a * a_run, a * b_run + b)
                h_i = a_run * h + b_run
                p_i = a_run * p
                b_ref[seg(i), :] = h_i
                a_ref[seg(i), :] = p_i
            h, p = h_i, p_i
        return h, p

    def scan_body(it, carry):
        hf, pf, hb, pb = carry
        for k0 in range(0, LRU_SCAN_UNROLL, LRU_SCAN_BATCH):
            i0 = it * LRU_SCAN_UNROLL + k0
            hf, pf = scan_steps(af_ref, bf_ref, [i0 + k for k in range(LRU_SCAN_BATCH)], hf, pf)
            hb, pb = scan_steps(ab_ref, bb_ref, [pitch - 1 - i0 - k for k in range(LRU_SCAN_BATCH)], hb, pb)
        return hf, pf, hb, pb

    zero = jnp.zeros((SUBLANES, LANES), F32)
    one = jnp.ones((SUBLANES, LANES), F32)
    hf, pf, hb, pb = lax.fori_loop(0, pitch // LRU_SCAN_UNROLL, scan_body, (zero, one, zero, one))

    row = lax.broadcasted_iota(jnp.int32, (SUBLANES, LANES), 0)
    cf = zero
    c = jnp.zeros((1, LANES), F32)
    for s in range(1, SUBLANES):
        c = pf[s - 1:s] * c + hf[s - 1:s]
        cf = jnp.where(row == s, c, cf)
    cbk = zero
    c = jnp.zeros((1, LANES), F32)
    for s in range(SUBLANES - 2, -1, -1):
        c = pb[s + 1:s + 2] * c + hb[s + 1:s + 2]
        cbk = jnp.where(row == s, c, cbk)
    carry_ref[0:SUBLANES, :] = cf
    carry_ref[SUBLANES:2 * SUBLANES, :] = cbk

    to = min(LRU_CHUNK, seq // SUBLANES)
    rows = lax.broadcasted_iota(jnp.int32, (to, LANES), 0)

    def out_body(c, carry):
        t0 = pl.multiple_of(c * to, to)
        s0 = lax.div(t0, pitch)
        s1 = jnp.minimum(s0 + 1, SUBLANES - 1)
        in_first = rows < (s0 + 1) * pitch - t0
        cf_rows = jnp.where(in_first, carry_ref[pl.ds(s0, 1), :], carry_ref[pl.ds(s1, 1), :])
        cb_rows = jnp.where(in_first, carry_ref[pl.ds(SUBLANES + s0, 1), :],
                            carry_ref[pl.ds(SUBLANES + s1, 1), :])
        sl = pl.ds(t0, to)
        h = bf_ref[sl, :] + af_ref[sl, :] * cf_rows + bb_ref[sl, :] + ab_ref[sl, :] * cb_rows
        o_ref[0, sl, :] = h * _gelu_tanh(yg_ref[0, sl, :])
        return carry

    lax.fori_loop(0, seq // to, out_body, 0)


def _lru(xr, yg, conv_w, conv_b, w_cat, b_cat, lam):
    b, seq, lw = xr.shape
    pitch = seq // SUBLANES
    while pitch % (2 * LRU_SCAN_UNROLL) != LRU_SCAN_UNROLL:
        pitch += 1
    bw = LRU_BLOCK_DIM
    seq_blk = pl.BlockSpec((1, seq, bw), lambda bi, n: (bi, 0, n))
    scan_rows = SUBLANES * pitch
    return pl.pallas_call(
        functools.partial(_lru_kernel, seq=seq, pitch=pitch),
        grid=(b, LRU_BLOCKS),
        in_specs=[
            seq_blk,
            seq_blk,
            pl.BlockSpec((LRU_CONV, bw), lambda bi, n: (0, n)),
            pl.BlockSpec((1, bw), lambda bi, n: (0, n)),
            pl.BlockSpec((1, bw, 4 * bw), lambda bi, n: (n, 0, 0)),
            pl.BlockSpec((1, 1, 4 * bw), lambda bi, n: (n, 0, 0)),
            pl.BlockSpec((1, 2, bw), lambda bi, n: (n, 0, 0)),
        ],
        out_specs=seq_blk,
        out_shape=jax.ShapeDtypeStruct((b, seq, lw), F32),
        scratch_shapes=[
            pltpu.VMEM((seq + 2 * SUBLANES, LANES), F32),
            pltpu.VMEM((scan_rows, LANES), F32),
            pltpu.VMEM((scan_rows, LANES), F32),
            pltpu.VMEM((scan_rows, LANES), F32),
            pltpu.VMEM((scan_rows, LANES), F32),
            pltpu.VMEM((2 * SUBLANES, LANES), F32),
        ],
        compiler_params=_cparams(("parallel", "parallel"), 52),
        name="rglru",
    )(xr, yg, conv_w, conv_b.reshape(1, lw), w_cat, b_cat, lam)


def _out_kernel(x_ref, attn_ref, lru_ref, mod_ref, ag_ref, lg_ref, w_ref, o_ref):
    aw = attn_ref.shape[1]
    na = _rms(attn_ref[...], ag_ref[...]).astype(BF16)
    nl = _rms(lru_ref[...], lg_ref[...]).astype(BF16)
    y = _dot(na, w_ref[0:aw, :]) + _dot(nl, w_ref[aw:, :])
    o_ref[...] = x_ref[...] + mod_ref[0][2:3] * y


def _out_proj(x2, attn2, lru2, mod6, row0, seq, attn_g, lru_g, w_out, layer):
    t, d = x2.shape
    aw = attn2.shape[1]
    lw = lru2.shape[1]
    tm = TOKEN_TILE
    tiles_per_seq = seq // tm
    tok = lambda w: pl.BlockSpec((tm, w), lambda i: (i, 0))
    const = lambda shape: pl.BlockSpec(shape, lambda i: (0,) * len(shape))
    return pl.pallas_call(
        _out_kernel,
        grid=(t // tm,),
        in_specs=[
            tok(d), tok(aw), tok(lw),
            pl.BlockSpec((1, 6, d), lambda i: (row0 + i // tiles_per_seq, 0, 0)),
            const((1, aw)), const((1, lw)),
            pl.BlockSpec((None,) + w_out.shape[1:], lambda i: (layer, 0, 0)),
        ],
        out_specs=tok(d),
        out_shape=jax.ShapeDtypeStruct((t, d), F32),
        compiler_params=_cparams(("parallel",), 48),
        name="out_proj",
    )(x2, attn2, lru2, mod6, attn_g.reshape(1, aw), lru_g.reshape(1, lw), w_out)


def _ffn_kernel(x_ref, xp_ref, xn_ref, mod_ref, g_ref, wg_ref, wu_ref, cw_ref, cb_ref, wd_ref, o_ref,
                h_ref, ge_ref, *, tiles_per_seq):
    i = pl.program_id(0)
    j = pl.program_id(1)
    tm = x_ref.shape[0]

    @pl.when(j == 0)
    def _():
        mod = mod_ref[0]

        def norm_mod(x):
            return _rms(x, g_ref[...]) * (1.0 + mod[4:5]) + mod[3:4]

        first = (i % tiles_per_seq) == 0
        last = (i % tiles_per_seq) == tiles_per_seq - 1
        h_ref[0:HALO, :] = jnp.where(first, 0.0, norm_mod(xp_ref[...])).astype(BF16)
        h_ref[HALO:HALO + tm, :] = norm_mod(x_ref[...]).astype(BF16)
        h_ref[HALO + tm:, :] = jnp.where(last, 0.0, norm_mod(xn_ref[...])).astype(BF16)
        o_ref[...] = jnp.zeros_like(o_ref)

    ge_ref[...] = _dot(h_ref[...], wg_ref[...])
    u = _dot(h_ref[HALO:HALO + tm, :], wu_ref[...])
    cw = cw_ref[...]
    g = ge_ref[HALO - 1:HALO - 1 + tm, :] * cw[0:1] + cb_ref[...]
    g = g + ge_ref[HALO:HALO + tm, :] * cw[1:2]
    g = g + ge_ref[HALO + 1:HALO + 1 + tm, :] * cw[2:3]
    act = (_gelu_tanh(g) * u).astype(BF16)
    o_ref[...] += _dot(act, wd_ref[...])

    @pl.when(j == pl.num_programs(1) - 1)
    def _():
        o_ref[...] = x_ref[...] + mod_ref[0][5:6] * o_ref[...]


def _ffn(x2, mod6, row0, seq, norm_g, w_gate, w_up, conv_w, conv_b, w_down, layer):
    t, d = x2.shape
    f = w_gate.shape[2]
    tm = TOKEN_TILE
    fc = FF_CHUNK
    tiles_per_seq = seq // tm
    hb = tm // HALO
    n_halo_blocks = t // HALO
    return pl.pallas_call(
        functools.partial(_ffn_kernel, tiles_per_seq=tiles_per_seq),
        grid=(t // tm, f // fc),
        in_specs=[
            pl.BlockSpec((tm, d), lambda i, j: (i, 0)),
            pl.BlockSpec((HALO, d), lambda i, j: (jnp.maximum(i * hb - 1, 0), 0)),
            pl.BlockSpec((HALO, d), lambda i, j: (jnp.minimum((i + 1) * hb, n_halo_blocks - 1), 0)),
            pl.BlockSpec((1, 6, d), lambda i, j: (row0 + i // tiles_per_seq, 0, 0)),
            pl.BlockSpec((1, d), lambda i, j: (0, 0)),
            pl.BlockSpec((None, d, fc), lambda i, j: (layer, 0, j)),
            pl.BlockSpec((None, d, fc), lambda i, j: (layer, 0, j)),
            pl.BlockSpec((FFN_CONV, fc), lambda i, j: (0, j)),
            pl.BlockSpec((1, fc), lambda i, j: (0, j)),
            pl.BlockSpec((None, fc, d), lambda i, j: (layer, j, 0)),
        ],
        out_specs=pl.BlockSpec((tm, d), lambda i, j: (i, 0)),
        out_shape=jax.ShapeDtypeStruct((t, d), F32),
        scratch_shapes=[pltpu.VMEM((tm + 2 * HALO, d), BF16), pltpu.VMEM((tm + 2 * HALO, fc), F32)],
        compiler_params=_cparams(("parallel", "arbitrary"), 48),
        name="convglu_ffn",
    )(x2, x2, x2, mod6, norm_g.reshape(1, d), w_gate, w_up, conv_w, conv_b.reshape(1, f), w_down)


def kernel(x_prompt, x_sample, c_prompt, c_sample, rel_bias, w_mod, b_mod, norm1_g, norm2_g, w_in, q_norm_g, k_norm_g, attn_sink, lru_conv_w, lru_conv_b, lru_w_a, lru_b_a, lru_w_x, lru_b_x, lru_lambda, attn_out_g, lru_out_g, w_out, ffn_w_gate, ffn_w_up, ffn_conv_w, ffn_conv_b, ffn_w_down):
    depth, d, _ = w_in.shape
    n_prompt = c_prompt.shape[0]
    n_req = n_prompt + c_sample.shape[0]
    rows = -(-n_req // SUBLANES) * SUBLANES
    c_all = jnp.concatenate([c_prompt, c_sample, jnp.zeros((rows - n_req, d), F32)], axis=0)
    mod = _modulation(c_all, w_mod, b_mod).reshape(depth, rows, 6, d)
    bias = _bias_table(rel_bias)

    w_in_b = w_in.astype(BF16)
    w_out_b = w_out.astype(BF16)
    w_gate_b = ffn_w_gate.astype(BF16)
    w_up_b = ffn_w_up.astype(BF16)
    w_down_b = ffn_w_down.astype(BF16)
    w_cat = jnp.concatenate([lru_w_a[:, 0], lru_w_a[:, 1], lru_w_x[:, 0], lru_w_x[:, 1]], axis=-1)
    w_cat = (0.5 * w_cat).astype(BF16)
    bw = LRU_BLOCK_DIM
    b_cat = jnp.concatenate(
        [lru_b_a.reshape(depth, 2, LRU_BLOCKS, bw), lru_b_x.reshape(depth, 2, LRU_BLOCKS, bw)], axis=1)
    b_cat = 0.5 * jnp.transpose(b_cat, (0, 2, 1, 3)).reshape(depth, LRU_BLOCKS, 1, 4 * bw)
    lam = jnp.transpose(lru_lambda.reshape(depth, 2, LRU_BLOCKS, bw), (0, 2, 1, 3))
    sink_rows = jnp.repeat(attn_sink.reshape(depth, N_KV_HEADS, GROUP), WINDOW, axis=-1)[..., None]
    sink_rows = jnp.broadcast_to(sink_rows, (depth, N_KV_HEADS, GROUP * WINDOW, HEAD_DIM))

    def run(x, row0):
        b, seq, _ = x.shape
        x2 = x.reshape(b * seq, d)
        for l in range(depth):
            q, k, v, xr, yg = _in_proj(x2, mod[l], row0, seq, norm1_g[l], w_in_b, l, q_norm_g[l], k_norm_g[l])
            sh = lambda a: a.reshape(b, seq, a.shape[-1])
            attn = _attention(sh(q), sh(k), sh(v), bias, sink_rows[l])
            lru = _lru(sh(xr), sh(yg), lru_conv_w[l], lru_conv_b[l], w_cat[l], b_cat[l], lam[l])
            x2 = _out_proj(x2, attn.reshape(b * seq, -1), lru.reshape(b * seq, -1), mod[l], row0, seq,
                           attn_out_g[l], lru_out_g[l], w_out_b, l)
            x2 = _ffn(x2, mod[l], row0, seq, norm2_g[l], w_gate_b, w_up_b, ffn_conv_w[l], ffn_conv_b[l],
                      w_down_b, l)
        return x2.reshape(b, seq, d)

    return (run(x_prompt, 0), run(x_sample, n_prompt))
```

```python
import functools
import math

import jax
import jax.numpy as jnp
from jax import lax
from jax.experimental import pallas as pl
from jax.experimental.pallas import tpu as pltpu

F32 = jnp.float32
BF16 = jnp.bfloat16

EPS = 1e-6
HEAD_DIM = 128
N_HEADS = 8
N_KV_HEADS = 2
GROUP = N_HEADS // N_KV_HEADS
WINDOW = 128
N_BUCKETS = 32
LRU_BLOCKS = 8
LRU_BLOCK_DIM = 128
LRU_CONV = 4
LRU_C = 8.0
FFN_CONV = 3
NEG_INF = -1e30
LOG2E = math.log2(math.e)
LN2 = math.log(2.0)
SERIES_LIMIT = 0.02
F32_TINY = float(jnp.finfo(jnp.float32).tiny)

SUBLANES = 8
LANES = 128
MIB = 1024 * 1024

TOKEN_TILE = 512
FF_CHUNK = 512
MOD_CHUNK = 1024
ATTN_Q_TILE = 1024
LRU_CHUNK = 512
LRU_SCAN_UNROLL = 12
LRU_SCAN_BATCH = 6
LRU_LOOKAHEAD = 3
ATTN_UNROLL = 4
HALO = SUBLANES


def _cparams(semantics, vmem_mib):
    return pltpu.CompilerParams(dimension_semantics=semantics, vmem_limit_bytes=vmem_mib * MIB)


def _dot(a, b):
    return jnp.dot(a, b, preferred_element_type=F32)


def _rms(x, gain):
    return x * lax.rsqrt(jnp.mean(x * x, axis=-1, keepdims=True) + EPS) * gain


def _gelu_tanh(x):
    return 0.5 * x * (1.0 + jnp.tanh(math.sqrt(2.0 / math.pi) * (x + 0.044715 * (x * x * x))))


def _sigmoid(x):
    return 1.0 / (1.0 + jnp.exp(-x))


def _one_minus_exp(x, exp_x):
    series = x * (-1.0 + x * (-1 / 2 + x * (-1 / 6 + x * (-1 / 24))))
    return jnp.where(x > -SERIES_LIMIT, series, 1.0 - exp_x)


def _mod_kernel(c_ref, w_ref, b_ref, o_ref):
    c = c_ref[...]
    cs = (c * _sigmoid(c)).astype(BF16)
    o_ref[0] = _dot(cs, w_ref[0].astype(BF16)) + b_ref[0]


def _modulation(c_all, w_mod, b_mod):
    depth, d, n = w_mod.shape
    rows = c_all.shape[0]
    return pl.pallas_call(
        _mod_kernel,
        grid=(depth, n // MOD_CHUNK),
        in_specs=[
            pl.BlockSpec((rows, d), lambda l, j: (0, 0)),
            pl.BlockSpec((1, d, MOD_CHUNK), lambda l, j: (l, 0, j)),
            pl.BlockSpec((1, 1, MOD_CHUNK), lambda l, j: (l, 0, j)),
        ],
        out_specs=pl.BlockSpec((1, rows, MOD_CHUNK), lambda l, j: (l, 0, j)),
        out_shape=jax.ShapeDtypeStruct((depth, rows, n), F32),
        compiler_params=_cparams(("arbitrary", "arbitrary"), 40),
        name="adaln_modulation",
    )(c_all, w_mod, b_mod.reshape(depth, 1, n))


_BUCKET_THRESHOLDS = (12, 16, 23, 32, 46, 64, 91)


def _bias_kernel(rb_ref, o_ref):
    head = pl.program_id(0)
    qi = lax.broadcasted_iota(jnp.int32, (WINDOW, 3 * WINDOW), 0)
    kj = lax.broadcasted_iota(jnp.int32, (WINDOW, 3 * WINDOW), 1)
    rel = kj - WINDOW - qi
    n = jnp.abs(rel)
    half = N_BUCKETS // 2
    large = jnp.full_like(n, half // 2)
    for t in _BUCKET_THRESHOLDS:
        large = large + jnp.where(n >= t, 1, 0)
    bucket = jnp.where(rel > 0, half, 0) + jnp.where(n < half // 2, n, large)
    bias = jnp.zeros((WINDOW, 3 * WINDOW), F32)
    for b in range(N_BUCKETS):
        bias = jnp.where(bucket == b, rb_ref[b, head], bias)
    o_ref[0] = jnp.where(n <= WINDOW, bias * LOG2E, NEG_INF)


def _bias_table(rel_bias):
    out = pl.pallas_call(
        _bias_kernel,
        grid=(N_HEADS,),
        in_specs=[pl.BlockSpec(memory_space=pltpu.SMEM)],
        out_specs=pl.BlockSpec((1, WINDOW, 3 * WINDOW), lambda h: (h, 0, 0)),
        out_shape=jax.ShapeDtypeStruct((N_HEADS, WINDOW, 3 * WINDOW), F32),
        compiler_params=_cparams(("arbitrary",), 16),
        name="rel_bias_table",
    )(rel_bias)
    return out.reshape(N_KV_HEADS, GROUP * WINDOW, 3 * WINDOW)


def _in_kernel(x_ref, mod_ref, g_ref, w_ref, qg_ref, kg_ref, q_ref, k_ref, v_ref, xr_ref, yg_ref):
    mod = mod_ref[0]
    h = _rms(x_ref[...], g_ref[...]) * (1.0 + mod[1:2]) + mod[0:1]
    hb = h.astype(BF16)
    nc = GROUP * HEAD_DIM
    for c in range(N_KV_HEADS):
        z = _dot(hb, w_ref[:, c * nc:(c + 1) * nc])
        for i in range(GROUP):
            zh = z[:, i * HEAD_DIM:(i + 1) * HEAD_DIM]
            q_ref[c, :, i * HEAD_DIM:(i + 1) * HEAD_DIM] = _rms(zh, qg_ref[...]).astype(BF16)
    aw = N_KV_HEADS * nc
    kvw = N_KV_HEADS * HEAD_DIM
    z = _dot(hb, w_ref[:, aw:aw + 2 * kvw])
    for i in range(N_KV_HEADS):
        k_ref[i] = _rms(z[:, i * HEAD_DIM:(i + 1) * HEAD_DIM], kg_ref[...]).astype(BF16)
        v_ref[i] = z[:, kvw + i * HEAD_DIM:kvw + (i + 1) * HEAD_DIM].astype(BF16)
    base = aw + 2 * kvw
    per = nc // LRU_BLOCK_DIM
    for dst_ref in (xr_ref, yg_ref):
        for c in range(LRU_BLOCKS // per):
            z = _dot(hb, w_ref[:, base + c * nc:base + (c + 1) * nc])
            for i in range(per):
                dst_ref[c * per + i] = z[:, i * LRU_BLOCK_DIM:(i + 1) * LRU_BLOCK_DIM]
        base = base + LRU_BLOCKS * LRU_BLOCK_DIM


def _in_proj(x2, mod6, row0, seq, norm_g, w_in, layer, q_g, k_g):
    t, d = x2.shape
    gw = GROUP * HEAD_DIM
    tm = TOKEN_TILE
    tiles_per_seq = seq // tm
    blocked = lambda n, w: pl.BlockSpec((n, tm, w), lambda i: (0, i, 0))
    const = lambda shape: pl.BlockSpec(shape, lambda i: (0,) * len(shape))
    return pl.pallas_call(
        _in_kernel,
        grid=(t // tm,),
        in_specs=[
            pl.BlockSpec((tm, d), lambda i: (i, 0)),
            pl.BlockSpec((1, 6, d), lambda i: (row0 + i // tiles_per_seq, 0, 0)),
            const((1, d)),
            pl.BlockSpec((None,) + w_in.shape[1:], lambda i: (layer, 0, 0)),
            const((1, HEAD_DIM)),
            const((1, HEAD_DIM)),
        ],
        out_specs=[blocked(N_KV_HEADS, gw), blocked(N_KV_HEADS, HEAD_DIM), blocked(N_KV_HEADS, HEAD_DIM),
                   blocked(LRU_BLOCKS, LRU_BLOCK_DIM), blocked(LRU_BLOCKS, LRU_BLOCK_DIM)],
        out_shape=[
            jax.ShapeDtypeStruct((N_KV_HEADS, t, gw), BF16),
            jax.ShapeDtypeStruct((N_KV_HEADS, t, HEAD_DIM), BF16),
            jax.ShapeDtypeStruct((N_KV_HEADS, t, HEAD_DIM), BF16),
            jax.ShapeDtypeStruct((LRU_BLOCKS, t, LRU_BLOCK_DIM), F32),
            jax.ShapeDtypeStruct((LRU_BLOCKS, t, LRU_BLOCK_DIM), F32),
        ],
        compiler_params=_cparams(("parallel",), 52),
        name="in_proj",
    )(x2, mod6, norm_g.reshape(1, d), w_in, q_g.reshape(1, HEAD_DIM), k_g.reshape(1, HEAD_DIM))


def _attn_kernel(q_ref, k_ref, v_ref, bias_ref, sink_ref, o_ref, *, seq):
    nb = seq // WINDOW
    blocks = q_ref.shape[0] // WINDOW
    chunk = pl.program_id(2)
    scale = HEAD_DIM ** -0.5 * LOG2E
    sink = sink_ref[0] * LOG2E
    col = lax.broadcasted_iota(jnp.int32, (1, 3 * WINDOW), 1)
    ones = jnp.ones((3 * WINDOW, HEAD_DIM), BF16)
    rows = GROUP * WINDOW

    def scores(j):
        n = chunk * blocks + j
        r0 = pl.multiple_of(j * WINDOW, WINDOW)
        qs = jnp.concatenate(
            [q_ref[pl.ds(r0, WINDOW), g * HEAD_DIM:(g + 1) * HEAD_DIM] for g in range(GROUP)], axis=0)
        starts = [pl.multiple_of(jnp.maximum(n - 1, 0) * WINDOW, WINDOW),
                  pl.multiple_of(n * WINDOW, WINDOW),
                  pl.multiple_of(jnp.minimum(n + 1, nb - 1) * WINDOW, WINDOW)]
        kw = jnp.concatenate([k_ref[pl.ds(s, WINDOW), :] for s in starts], axis=0)
        vw = jnp.concatenate(
            [jnp.concatenate([v_ref[pl.ds(s, WINDOW), :] for s in starts], axis=0), ones], axis=1)
        s = lax.dot_general(qs, kw, (((1,), (1,)), ((), ())), preferred_element_type=F32)
        return n, r0, s, vw

    def softmax(n, r0, s, vw):
        s = s * scale + bias_ref[0]
        valid = ((col >= WINDOW) | (n > 0)) & ((col < 2 * WINDOW) | (n < nb - 1))
        s = jnp.where(valid, s, NEG_INF)
        m = jnp.maximum(jnp.broadcast_to(jnp.max(s, axis=-1, keepdims=True), (rows, HEAD_DIM)), sink)
        p = jnp.exp2(s - jnp.concatenate([m, m, m], axis=1)).astype(BF16)
        return r0, p, jnp.exp2(sink - m), vw

    def finish(r0, p, sink_term, vw):
        ov = _dot(p, vw)
        o = ov[:, :HEAD_DIM] / (ov[:, HEAD_DIM:] + sink_term)
        for g in range(GROUP):
            o_ref[pl.ds(r0, WINDOW), g * HEAD_DIM:(g + 1) * HEAD_DIM] = o[g * WINDOW:(g + 1) * WINDOW]

    def body(jj, carry):
        stage = [scores(jj * ATTN_UNROLL + u) for u in range(ATTN_UNROLL)]
        stage = [softmax(*st) for st in stage]
        for st in stage:
            finish(*st)
        return carry

    lax.fori_loop(0, blocks // ATTN_UNROLL, body, 0)


def _attention(q, k, v, bias, sink_rows):
    _, b, seq, gw = q.shape
    tq = min(ATTN_Q_TILE, seq)
    seq_blk = pl.BlockSpec((None, None, seq, HEAD_DIM), lambda bi, h, c: (h, bi, 0, 0))
    return pl.pallas_call(
        functools.partial(_attn_kernel, seq=seq),
        grid=(b, N_KV_HEADS, seq // tq),
        in_specs=[
            pl.BlockSpec((None, None, tq, gw), lambda bi, h, c: (h, bi, c, 0)),
            seq_blk,
            seq_blk,
            pl.BlockSpec((1, GROUP * WINDOW, 3 * WINDOW), lambda bi, h, c: (h, 0, 0)),
            pl.BlockSpec((1, GROUP * WINDOW, HEAD_DIM), lambda bi, h, c: (h, 0, 0)),
        ],
        out_specs=pl.BlockSpec((None, None, tq, gw), lambda bi, h, c: (h, bi, c, 0)),
        out_shape=jax.ShapeDtypeStruct((N_KV_HEADS, b, seq, gw), F32),
        compiler_params=_cparams(("parallel", "parallel", "arbitrary"), 40),
        name="window_attention",
    )(q, k, v, bias, sink_rows)


def _lru_kernel(xr_ref, yg_ref, cw_ref, cb_ref, w_ref, gb_ref, lam_ref, o_ref,
                xp_ref, af_ref, bf_ref, ab_ref, bb_ref, carry_ref, *, seq, pitch):
    tc = min(LRU_CHUNK, seq)
    n_chunks = seq // tc
    pad = SUBLANES
    zeros8 = jnp.zeros((pad, LANES), F32)
    xp_ref[0:pad, :] = zeros8
    xp_ref[seq + pad:seq + 2 * pad, :] = zeros8
    tail = SUBLANES * pitch - seq
    for r in (af_ref, bf_ref, ab_ref, bb_ref):
        r[seq:seq + tail, :] = jnp.zeros((tail, LANES), F32)

    def copy_body(c, carry):
        t0 = pl.multiple_of(c * tc, tc)
        xp_ref[pl.ds(t0 + pad, tc), :] = xr_ref[pl.ds(t0, tc), :]
        return carry

    lax.fori_loop(0, n_chunks, copy_body, 0)

    half_log2_decay = (-0.5 * LRU_C * LOG2E) * jax.nn.softplus(-lam_ref[0])
    cw = cw_ref[...]
    cb = cb_ref[...]
    gb = gb_ref[0]
    bw = LRU_BLOCK_DIM

    def gate_body(c, carry):
        t0 = pl.multiple_of(c * tc, tc)
        xc = xp_ref[pl.ds(t0 + pad - 2, tc), :] * cw[0:1] + cb
        for kk in range(1, LRU_CONV):
            xc = xc + xp_ref[pl.ds(t0 + pad - 2 + kk, tc), :] * cw[kk:kk + 1]
        t = jnp.tanh(_dot(xc.astype(BF16), w_ref[0]) + gb)
        xh = 0.5 * xc
        for d, (a_ref, b_ref) in enumerate(((af_ref, bf_ref), (ab_ref, bb_ref))):
            log2_a = (1.0 + t[:, d * bw:(d + 1) * bw]) * half_log2_decay[d:d + 1]
            a = jnp.exp2(log2_a)
            a_ref[pl.ds(t0, tc), :] = a
            y = _one_minus_exp(log2_a * (2.0 * LN2), a * a)
            root = y * lax.rsqrt(jnp.maximum(y, F32_TINY))
            b_ref[pl.ds(t0, tc), :] = root * (1.0 + t[:, (2 + d) * bw:(3 + d) * bw]) * xh
        return carry

    lax.fori_loop(0, n_chunks, gate_body, 0)

    def seg(i):
        return pl.ds(i, SUBLANES, stride=pitch)

    def scan_steps(a_ref, b_ref, steps, h, p):
        ab = [(a_ref[seg(i), :], b_ref[seg(i), :]) for i in steps]
        for g0 in range(0, len(steps), LRU_LOOKAHEAD):
            a_run = b_run = None
            for i, (a, b) in zip(steps[g0:g0 + LRU_LOOKAHEAD], ab[g0:g0 + LRU_LOOKAHEAD]):
                a_run, b_run = (a, b) if a_run is None else (a * a_run, a * b_run + b)
                h_i = a_run * h + b_run
                p_i = a_run * p
                b_ref[seg(i), :] = h_i
                a_ref[seg(i), :] = p_i
            h, p = h_i, p_i
        return h, p

    def scan_body(it, carry):
        hf, pf, hb, pb = carry
        for k0 in range(0, LRU_SCAN_UNROLL, LRU_SCAN_BATCH):
            i0 = it * LRU_SCAN_UNROLL + k0
            hf, pf = scan_steps(af_ref, bf_ref, [i0 + k for k in range(LRU_SCAN_BATCH)], hf, pf)
            hb, pb = scan_steps(ab_ref, bb_ref, [pitch - 1 - i0 - k for k in range(LRU_SCAN_BATCH)], hb, pb)
        return hf, pf, hb, pb

    zero = jnp.zeros((SUBLANES, LANES), F32)
    one = jnp.ones((SUBLANES, LANES), F32)
    hf, pf, hb, pb = lax.fori_loop(0, pitch // LRU_SCAN_UNROLL, scan_body, (zero, one, zero, one))

    row = lax.broadcasted_iota(jnp.int32, (SUBLANES, LANES), 0)
    cf = zero
    c = jnp.zeros((1, LANES), F32)
    for s in range(1, SUBLANES):
        c = pf[s - 1:s] * c + hf[s - 1:s]
        cf = jnp.where(row == s, c, cf)
    cbk = zero
    c = jnp.zeros((1, LANES), F32)
    for s in range(SUBLANES - 2, -1, -1):
        c = pb[s + 1:s + 2] * c + hb[s + 1:s + 2]
        cbk = jnp.where(row == s, c, cbk)
    carry_ref[0:SUBLANES, :] = cf
    carry_ref[SUBLANES:2 * SUBLANES, :] = cbk

    to = min(LRU_CHUNK, seq // SUBLANES)
    rows = lax.broadcasted_iota(jnp.int32, (to, LANES), 0)

    def out_body(c, carry):
        t0 = pl.multiple_of(c * to, to)
        s0 = lax.div(t0, pitch)
        s1 = jnp.minimum(s0 + 1, SUBLANES - 1)
        in_first = rows < (s0 + 1) * pitch - t0
        cf_rows = jnp.where(in_first, carry_ref[pl.ds(s0, 1), :], carry_ref[pl.ds(s1, 1), :])
        cb_rows = jnp.where(in_first, carry_ref[pl.ds(SUBLANES + s0, 1), :],
                            carry_ref[pl.ds(SUBLANES + s1, 1), :])
        sl = pl.ds(t0, to)
        h = bf_ref[sl, :] + af_ref[sl, :] * cf_rows + bb_ref[sl, :] + ab_ref[sl, :] * cb_rows
        o_ref[sl, :] = h * _gelu_tanh(yg_ref[sl, :])
        return carry

    lax.fori_loop(0, seq // to, out_body, 0)


def _lru(xr, yg, conv_w, conv_b, w_cat, b_cat, lam):
    _, b, seq, bw = xr.shape
    lw = LRU_BLOCKS * bw
    pitch = seq // SUBLANES
    while pitch % (2 * LRU_SCAN_UNROLL) != LRU_SCAN_UNROLL:
        pitch += 1
    seq_blk = pl.BlockSpec((None, None, seq, bw), lambda bi, n: (n, bi, 0, 0))
    scan_rows = SUBLANES * pitch
    return pl.pallas_call(
        functools.partial(_lru_kernel, seq=seq, pitch=pitch),
        grid=(b, LRU_BLOCKS),
        in_specs=[
            seq_blk,
            seq_blk,
            pl.BlockSpec((LRU_CONV, bw), lambda bi, n: (0, n)),
            pl.BlockSpec((1, bw), lambda bi, n: (0, n)),
            pl.BlockSpec((1, bw, 4 * bw), lambda bi, n: (n, 0, 0)),
            pl.BlockSpec((1, 1, 4 * bw), lambda bi, n: (n, 0, 0)),
            pl.BlockSpec((1, 2, bw), lambda bi, n: (n, 0, 0)),
        ],
        out_specs=seq_blk,
        out_shape=jax.ShapeDtypeStruct((LRU_BLOCKS, b, seq, bw), F32),
        scratch_shapes=[
            pltpu.VMEM((seq + 2 * SUBLANES, LANES), F32),
            pltpu.VMEM((scan_rows, LANES), F32),
            pltpu.VMEM((scan_rows, LANES), F32),
            pltpu.VMEM((scan_rows, LANES), F32),
            pltpu.VMEM((scan_rows, LANES), F32),
            pltpu.VMEM((2 * SUBLANES, LANES), F32),
        ],
        compiler_params=_cparams(("parallel", "parallel"), 52),
        name="rglru",
    )(xr, yg, conv_w, conv_b.reshape(1, lw), w_cat, b_cat, lam)


def _out_kernel(x_ref, attn_ref, lru_ref, mod_ref, ag_ref, lg_ref, w_ref, o_ref):
    attn = jnp.concatenate([attn_ref[c] for c in range(attn_ref.shape[0])], axis=1)
    lru = jnp.concatenate([lru_ref[c] for c in range(lru_ref.shape[0])], axis=1)
    aw = attn.shape[1]
    na = _rms(attn, ag_ref[...]).astype(BF16)
    nl = _rms(lru, lg_ref[...]).astype(BF16)
    y = _dot(na, w_ref[0:aw, :]) + _dot(nl, w_ref[aw:, :])
    o_ref[...] = x_ref[...] + mod_ref[0][2:3] * y


def _out_proj(x2, attn3, lru3, mod6, row0, seq, attn_g, lru_g, w_out, layer):
    t, d = x2.shape
    aw = attn3.shape[0] * attn3.shape[2]
    lw = lru3.shape[0] * lru3.shape[2]
    tm = TOKEN_TILE
    tiles_per_seq = seq // tm
    tok = lambda w: pl.BlockSpec((tm, w), lambda i: (i, 0))
    blocked = lambda a: pl.BlockSpec((a.shape[0], tm, a.shape[2]), lambda i: (0, i, 0))
    const = lambda shape: pl.BlockSpec(shape, lambda i: (0,) * len(shape))
    return pl.pallas_call(
        _out_kernel,
        grid=(t // tm,),
        in_specs=[
            tok(d), blocked(attn3), blocked(lru3),
            pl.BlockSpec((1, 6, d), lambda i: (row0 + i // tiles_per_seq, 0, 0)),
            const((1, aw)), const((1, lw)),
            pl.BlockSpec((None,) + w_out.shape[1:], lambda i: (layer, 0, 0)),
        ],
        out_specs=tok(d),
        out_shape=jax.ShapeDtypeStruct((t, d), F32),
        compiler_params=_cparams(("parallel",), 48),
        name="out_proj",
    )(x2, attn3, lru3, mod6, attn_g.reshape(1, aw), lru_g.reshape(1, lw), w_out)


def _ffn_kernel(x_ref, xp_ref, xn_ref, mod_ref, g_ref, wg_ref, wu_ref, cw_ref, cb_ref, wd_ref, o_ref,
                h_ref, ge_ref, *, tiles_per_seq):
    i = pl.program_id(0)
    j = pl.program_id(1)
    tm = x_ref.shape[0]

    @pl.when(j == 0)
    def _():
        mod = mod_ref[0]

        def norm_mod(x):
            return _rms(x, g_ref[...]) * (1.0 + mod[4:5]) + mod[3:4]

        first = (i % tiles_per_seq) == 0
        last = (i % tiles_per_seq) == tiles_per_seq - 1
        h_ref[0:HALO, :] = jnp.where(first, 0.0, norm_mod(xp_ref[...])).astype(BF16)
        h_ref[HALO:HALO + tm, :] = norm_mod(x_ref[...]).astype(BF16)
        h_ref[HALO + tm:, :] = jnp.where(last, 0.0, norm_mod(xn_ref[...])).astype(BF16)
        o_ref[...] = jnp.zeros_like(o_ref)

    ge_ref[...] = _dot(h_ref[...], wg_ref[...])
    u = _dot(h_ref[HALO:HALO + tm, :], wu_ref[...])
    cw = cw_ref[...]
    g = ge_ref[HALO - 1:HALO - 1 + tm, :] * cw[0:1] + cb_ref[...]
    g = g + ge_ref[HALO:HALO + tm, :] * cw[1:2]
    g = g + ge_ref[HALO + 1:HALO + 1 + tm, :] * cw[2:3]
    act = (_gelu_tanh(g) * u).astype(BF16)
    o_ref[...] += _dot(act, wd_ref[...])

    @pl.when(j == pl.num_programs(1) - 1)
    def _():
        o_ref[...] = x_ref[...] + mod_ref[0][5:6] * o_ref[...]


def _ffn(x2, mod6, row0, seq, norm_g, w_gate, w_up, conv_w, conv_b, w_down, layer):
    t, d = x2.shape
    fc = w_gate.shape[3]
    f = w_gate.shape[1] * fc
    tm = TOKEN_TILE
    tiles_per_seq = seq // tm
    hb = tm // HALO
    n_halo_blocks = t // HALO
    return pl.pallas_call(
        functools.partial(_ffn_kernel, tiles_per_seq=tiles_per_seq),
        grid=(t // tm, f // fc),
        in_specs=[
            pl.BlockSpec((tm, d), lambda i, j: (i, 0)),
            pl.BlockSpec((HALO, d), lambda i, j: (jnp.maximum(i * hb - 1, 0), 0)),
            pl.BlockSpec((HALO, d), lambda i, j: (jnp.minimum((i + 1) * hb, n_halo_blocks - 1), 0)),
            pl.BlockSpec((1, 6, d), lambda i, j: (row0 + i // tiles_per_seq, 0, 0)),
            pl.BlockSpec((1, d), lambda i, j: (0, 0)),
            pl.BlockSpec((None, None, d, fc), lambda i, j: (layer, j, 0, 0)),
            pl.BlockSpec((None, None, d, fc), lambda i, j: (layer, j, 0, 0)),
            pl.BlockSpec((FFN_CONV, fc), lambda i, j: (0, j)),
            pl.BlockSpec((1, fc), lambda i, j: (0, j)),
            pl.BlockSpec((None, fc, d), lambda i, j: (layer, j, 0)),
        ],
        out_specs=pl.BlockSpec((tm, d), lambda i, j: (i, 0)),
        out_shape=jax.ShapeDtypeStruct((t, d), F32),
        scratch_shapes=[pltpu.VMEM((tm + 2 * HALO, d), BF16), pltpu.VMEM((tm + 2 * HALO, fc), F32)],
        compiler_params=_cparams(("parallel", "arbitrary"), 48),
        name="convglu_ffn",
    )(x2, x2, x2, mod6, norm_g.reshape(1, d), w_gate, w_up, conv_w, conv_b.reshape(1, f), w_down)


def kernel(x_prompt, x_sample, c_prompt, c_sample, rel_bias, w_mod, b_mod, norm1_g, norm2_g, w_in, q_norm_g, k_norm_g, attn_sink, lru_conv_w, lru_conv_b, lru_w_a, lru_b_a, lru_w_x, lru_b_x, lru_lambda, attn_out_g, lru_out_g, w_out, ffn_w_gate, ffn_w_up, ffn_conv_w, ffn_conv_b, ffn_w_down):
    depth, d, _ = w_in.shape
    n_prompt = c_prompt.shape[0]
    n_req = n_prompt + c_sample.shape[0]
    rows = -(-n_req // SUBLANES) * SUBLANES
    c_all = jnp.concatenate([c_prompt, c_sample, jnp.zeros((rows - n_req, d), F32)], axis=0)
    mod = _modulation(c_all, w_mod, b_mod).reshape(depth, rows, 6, d)
    bias = _bias_table(rel_bias)

    w_in_b = w_in.astype(BF16)
    w_out_b = w_out.astype(BF16)
    chunked = lambda w: jnp.transpose(w.astype(BF16).reshape(depth, d, -1, FF_CHUNK), (0, 2, 1, 3))
    w_gate_b = chunked(ffn_w_gate)
    w_up_b = chunked(ffn_w_up)
    w_down_b = ffn_w_down.astype(BF16)
    w_cat = jnp.concatenate([lru_w_a[:, 0], lru_w_a[:, 1], lru_w_x[:, 0], lru_w_x[:, 1]], axis=-1)
    w_cat = (0.5 * w_cat).astype(BF16)
    bw = LRU_BLOCK_DIM
    b_cat = jnp.concatenate(
        [lru_b_a.reshape(depth, 2, LRU_BLOCKS, bw), lru_b_x.reshape(depth, 2, LRU_BLOCKS, bw)], axis=1)
    b_cat = 0.5 * jnp.transpose(b_cat, (0, 2, 1, 3)).reshape(depth, LRU_BLOCKS, 1, 4 * bw)
    lam = jnp.transpose(lru_lambda.reshape(depth, 2, LRU_BLOCKS, bw), (0, 2, 1, 3))
    sink_rows = jnp.repeat(attn_sink.reshape(depth, N_KV_HEADS, GROUP), WINDOW, axis=-1)[..., None]
    sink_rows = jnp.broadcast_to(sink_rows, (depth, N_KV_HEADS, GROUP * WINDOW, HEAD_DIM))

    def run(x, row0):
        b, seq, _ = x.shape
        x2 = x.reshape(b * seq, d)
        for l in range(depth):
            q, k, v, xr, yg = _in_proj(x2, mod[l], row0, seq, norm1_g[l], w_in_b, l, q_norm_g[l], k_norm_g[l])
            sh = lambda a: a.reshape(a.shape[0], b, seq, a.shape[-1])
            flat = lambda a: a.reshape(a.shape[0], b * seq, a.shape[-1])
            attn = _attention(sh(q), sh(k), sh(v), bias, sink_rows[l])
            lru = _lru(sh(xr), sh(yg), lru_conv_w[l], lru_conv_b[l], w_cat[l], b_cat[l], lam[l])
            x2 = _out_proj(x2, flat(attn), flat(lru), mod[l], row0, seq,
                           attn_out_g[l], lru_out_g[l], w_out_b, l)
            x2 = _ffn(x2, mod[l], row0, seq, norm2_g[l], w_gate_b, w_up_b, ffn_conv_w[l], ffn_conv_b[l],
                      w_down_b, l)
        return x2.reshape(b, seq, d)

    return (run(x_prompt, 0), run(x_sample, n_prompt))
```

```python
import functools
import math

import jax
import jax.numpy as jnp
from jax import lax
from jax.experimental import pallas as pl
from jax.experimental.pallas import tpu as pltpu

F32 = jnp.float32
BF16 = jnp.bfloat16

EPS = 1e-6
HEAD_DIM = 128
N_HEADS = 8
N_KV_HEADS = 2
GROUP = N_HEADS // N_KV_HEADS
WINDOW = 128
N_BUCKETS = 32
LRU_BLOCKS = 8
LRU_BLOCK_DIM = 128
LRU_CONV = 4
LRU_C = 8.0
FFN_CONV = 3
NEG_INF = -1e30
LOG2E = math.log2(math.e)
LN2 = math.log(2.0)
SERIES_LIMIT = 0.02
F32_TINY = float(jnp.finfo(jnp.float32).tiny)

SUBLANES = 8
LANES = 128
MIB = 1024 * 1024

TOKEN_TILE = 512
FFN_TOKEN_TILE = 512
FF_CHUNK = 1024
MOD_CHUNK = 1024
ATTN_Q_TILE = 1024
LRU_CHUNK = 512
LRU_SCAN_UNROLL = 12
LRU_SCAN_BATCH = 6
LRU_LOOKAHEAD = 3
ATTN_UNROLL = 4
HALO = SUBLANES


def _cparams(semantics, vmem_mib):
    return pltpu.CompilerParams(dimension_semantics=semantics, vmem_limit_bytes=vmem_mib * MIB)


def _dot(a, b):
    return jnp.dot(a, b, preferred_element_type=F32)


def _rms(x, gain):
    return x * lax.rsqrt(jnp.mean(x * x, axis=-1, keepdims=True) + EPS) * gain


def _gelu_tanh(x):
    return 0.5 * x * (1.0 + jnp.tanh(math.sqrt(2.0 / math.pi) * (x + 0.044715 * (x * x * x))))


def _sigmoid(x):
    return 1.0 / (1.0 + jnp.exp(-x))


def _one_minus_exp(x, exp_x):
    series = x * (-1.0 + x * (-1 / 2 + x * (-1 / 6 + x * (-1 / 24))))
    return jnp.where(x > -SERIES_LIMIT, series, 1.0 - exp_x)


def _mod_kernel(c_ref, w_ref, b_ref, o_ref):
    c = c_ref[...]
    cs = (c * _sigmoid(c)).astype(BF16)
    o_ref[0] = _dot(cs, w_ref[0].astype(BF16)) + b_ref[0]


def _modulation(c_all, w_mod, b_mod):
    depth, d, n = w_mod.shape
    rows = c_all.shape[0]
    return pl.pallas_call(
        _mod_kernel,
        grid=(depth, n // MOD_CHUNK),
        in_specs=[
            pl.BlockSpec((rows, d), lambda l, j: (0, 0)),
            pl.BlockSpec((1, d, MOD_CHUNK), lambda l, j: (l, 0, j)),
            pl.BlockSpec((1, 1, MOD_CHUNK), lambda l, j: (l, 0, j)),
        ],
        out_specs=pl.BlockSpec((1, rows, MOD_CHUNK), lambda l, j: (l, 0, j)),
        out_shape=jax.ShapeDtypeStruct((depth, rows, n), F32),
        compiler_params=_cparams(("arbitrary", "arbitrary"), 40),
        name="adaln_modulation",
    )(c_all, w_mod, b_mod.reshape(depth, 1, n))


_BUCKET_THRESHOLDS = (12, 16, 23, 32, 46, 64, 91)


def _bias_kernel(rb_ref, o_ref):
    head = pl.program_id(0)
    qi = lax.broadcasted_iota(jnp.int32, (WINDOW, 3 * WINDOW), 0)
    kj = lax.broadcasted_iota(jnp.int32, (WINDOW, 3 * WINDOW), 1)
    rel = kj - WINDOW - qi
    n = jnp.abs(rel)
    half = N_BUCKETS // 2
    large = jnp.full_like(n, half // 2)
    for t in _BUCKET_THRESHOLDS:
        large = large + jnp.where(n >= t, 1, 0)
    bucket = jnp.where(rel > 0, half, 0) + jnp.where(n < half // 2, n, large)
    bias = jnp.zeros((WINDOW, 3 * WINDOW), F32)
    for b in range(N_BUCKETS):
        bias = jnp.where(bucket == b, rb_ref[b, head], bias)
    o_ref[0] = jnp.where(n <= WINDOW, bias * LOG2E, NEG_INF)


def _bias_table(rel_bias):
    out = pl.pallas_call(
        _bias_kernel,
        grid=(N_HEADS,),
        in_specs=[pl.BlockSpec(memory_space=pltpu.SMEM)],
        out_specs=pl.BlockSpec((1, WINDOW, 3 * WINDOW), lambda h: (h, 0, 0)),
        out_shape=jax.ShapeDtypeStruct((N_HEADS, WINDOW, 3 * WINDOW), F32),
        compiler_params=_cparams(("arbitrary",), 16),
        name="rel_bias_table",
    )(rel_bias)
    return out.reshape(N_KV_HEADS, GROUP * WINDOW, 3 * WINDOW)


def _in_kernel(x_ref, mod_ref, g_ref, w_ref, qg_ref, kg_ref, q_ref, k_ref, v_ref, xr_ref, yg_ref):
    mod = mod_ref[0]
    h = _rms(x_ref[...], g_ref[...]) * (1.0 + mod[1:2]) + mod[0:1]
    hb = h.astype(BF16)
    nc = GROUP * HEAD_DIM
    for c in range(N_KV_HEADS):
        z = _dot(hb, w_ref[:, c * nc:(c + 1) * nc])
        for i in range(GROUP):
            zh = z[:, i * HEAD_DIM:(i + 1) * HEAD_DIM]
            q_ref[c, :, i * HEAD_DIM:(i + 1) * HEAD_DIM] = _rms(zh, qg_ref[...]).astype(BF16)
    aw = N_KV_HEADS * nc
    kvw = N_KV_HEADS * HEAD_DIM
    z = _dot(hb, w_ref[:, aw:aw + 2 * kvw])
    for i in range(N_KV_HEADS):
        k_ref[i] = _rms(z[:, i * HEAD_DIM:(i + 1) * HEAD_DIM], kg_ref[...]).astype(BF16)
        v_ref[i] = z[:, kvw + i * HEAD_DIM:kvw + (i + 1) * HEAD_DIM].astype(BF16)
    base = aw + 2 * kvw
    per = nc // LRU_BLOCK_DIM
    for dst_ref in (xr_ref, yg_ref):
        for c in range(LRU_BLOCKS // per):
            z = _dot(hb, w_ref[:, base + c * nc:base + (c + 1) * nc])
            for i in range(per):
                dst_ref[c * per + i] = z[:, i * LRU_BLOCK_DIM:(i + 1) * LRU_BLOCK_DIM]
        base = base + LRU_BLOCKS * LRU_BLOCK_DIM


def _in_proj(x2, mod6, row0, seq, norm_g, w_in, layer, q_g, k_g):
    t, d = x2.shape
    gw = GROUP * HEAD_DIM
    tm = TOKEN_TILE
    tiles_per_seq = seq // tm
    blocked = lambda n, w: pl.BlockSpec((n, tm, w), lambda i: (0, i, 0))
    const = lambda shape: pl.BlockSpec(shape, lambda i: (0,) * len(shape))
    return pl.pallas_call(
        _in_kernel,
        grid=(t // tm,),
        in_specs=[
            pl.BlockSpec((tm, d), lambda i: (i, 0)),
            pl.BlockSpec((1, 6, d), lambda i: (row0 + i // tiles_per_seq, 0, 0)),
            const((1, d)),
            pl.BlockSpec((None,) + w_in.shape[1:], lambda i: (layer, 0, 0)),
            const((1, HEAD_DIM)),
            const((1, HEAD_DIM)),
        ],
        out_specs=[blocked(N_KV_HEADS, gw), blocked(N_KV_HEADS, HEAD_DIM), blocked(N_KV_HEADS, HEAD_DIM),
                   blocked(LRU_BLOCKS, LRU_BLOCK_DIM), blocked(LRU_BLOCKS, LRU_BLOCK_DIM)],
        out_shape=[
            jax.ShapeDtypeStruct((N_KV_HEADS, t, gw), BF16),
            jax.ShapeDtypeStruct((N_KV_HEADS, t, HEAD_DIM), BF16),
            jax.ShapeDtypeStruct((N_KV_HEADS, t, HEAD_DIM), BF16),
            jax.ShapeDtypeStruct((LRU_BLOCKS, t, LRU_BLOCK_DIM), F32),
            jax.ShapeDtypeStruct((LRU_BLOCKS, t, LRU_BLOCK_DIM), F32),
        ],
        compiler_params=_cparams(("parallel",), 52),
        name="in_proj",
    )(x2, mod6, norm_g.reshape(1, d), w_in, q_g.reshape(1, HEAD_DIM), k_g.reshape(1, HEAD_DIM))


def _attn_kernel(q_ref, k_ref, v_ref, bias_ref, sink_ref, o_ref, *, seq):
    nb = seq // WINDOW
    blocks = q_ref.shape[0] // WINDOW
    chunk = pl.program_id(2)
    scale = HEAD_DIM ** -0.5 * LOG2E
    sink = sink_ref[0] * LOG2E
    col = lax.broadcasted_iota(jnp.int32, (1, 3 * WINDOW), 1)
    ones = jnp.ones((3 * WINDOW, HEAD_DIM), BF16)
    rows = GROUP * WINDOW

    def scores(j):
        n = chunk * blocks + j
        r0 = pl.multiple_of(j * WINDOW, WINDOW)
        qs = jnp.concatenate(
            [q_ref[pl.ds(r0, WINDOW), g * HEAD_DIM:(g + 1) * HEAD_DIM] for g in range(GROUP)], axis=0)
        starts = [pl.multiple_of(jnp.maximum(n - 1, 0) * WINDOW, WINDOW),
                  pl.multiple_of(n * WINDOW, WINDOW),
                  pl.multiple_of(jnp.minimum(n + 1, nb - 1) * WINDOW, WINDOW)]
        kw = jnp.concatenate([k_ref[pl.ds(s, WINDOW), :] for s in starts], axis=0)
        vw = jnp.concatenate(
            [jnp.concatenate([v_ref[pl.ds(s, WINDOW), :] for s in starts], axis=0), ones], axis=1)
        s = lax.dot_general(qs, kw, (((1,), (1,)), ((), ())), preferred_element_type=F32)
        return n, r0, s, vw

    def softmax(n, r0, s, vw):
        s = s * scale + bias_ref[0]
        valid = ((col >= WINDOW) | (n > 0)) & ((col < 2 * WINDOW) | (n < nb - 1))
        s = jnp.where(valid, s, NEG_INF)
        m = jnp.maximum(jnp.broadcast_to(jnp.max(s, axis=-1, keepdims=True), (rows, HEAD_DIM)), sink)
        p = jnp.exp2(s - jnp.concatenate([m, m, m], axis=1)).astype(BF16)
        return r0, p, jnp.exp2(sink - m), vw

    def finish(r0, p, sink_term, vw):
        ov = _dot(p, vw)
        o = ov[:, :HEAD_DIM] / (ov[:, HEAD_DIM:] + sink_term)
        for g in range(GROUP):
            o_ref[pl.ds(r0, WINDOW), g * HEAD_DIM:(g + 1) * HEAD_DIM] = o[g * WINDOW:(g + 1) * WINDOW]

    def body(jj, carry):
        stage = [scores(jj * ATTN_UNROLL + u) for u in range(ATTN_UNROLL)]
        stage = [softmax(*st) for st in stage]
        for st in stage:
            finish(*st)
        return carry

    lax.fori_loop(0, blocks // ATTN_UNROLL, body, 0)


def _attention(q, k, v, bias, sink_rows):
    _, b, seq, gw = q.shape
    tq = min(ATTN_Q_TILE, seq)
    seq_blk = pl.BlockSpec((None, None, seq, HEAD_DIM), lambda bi, h, c: (h, bi, 0, 0))
    return pl.pallas_call(
        functools.partial(_attn_kernel, seq=seq),
        grid=(b, N_KV_HEADS, seq // tq),
        in_specs=[
            pl.BlockSpec((None, None, tq, gw), lambda bi, h, c: (h, bi, c, 0)),
            seq_blk,
            seq_blk,
            pl.BlockSpec((1, GROUP * WINDOW, 3 * WINDOW), lambda bi, h, c: (h, 0, 0)),
            pl.BlockSpec((1, GROUP * WINDOW, HEAD_DIM), lambda bi, h, c: (h, 0, 0)),
        ],
        out_specs=pl.BlockSpec((None, None, tq, gw), lambda bi, h, c: (h, bi, c, 0)),
        out_shape=jax.ShapeDtypeStruct((N_KV_HEADS, b, seq, gw), F32),
        compiler_params=_cparams(("parallel", "parallel", "arbitrary"), 40),
        name="window_attention",
    )(q, k, v, bias, sink_rows)


def _lru_kernel(xr_ref, yg_ref, cw_ref, cb_ref, w_ref, gb_ref, lam_ref, o_ref,
                xp_ref, af_ref, bf_ref, ab_ref, bb_ref, carry_ref, *, seq, pitch):
    tc = min(LRU_CHUNK, seq)
    n_chunks = seq // tc
    pad = SUBLANES
    zeros8 = jnp.zeros((pad, LANES), F32)
    xp_ref[0:pad, :] = zeros8
    xp_ref[seq + pad:seq + 2 * pad, :] = zeros8
    tail = SUBLANES * pitch - seq
    for r in (af_ref, bf_ref, ab_ref, bb_ref):
        r[seq:seq + tail, :] = jnp.zeros((tail, LANES), F32)

    def copy_body(c, carry):
        t0 = pl.multiple_of(c * tc, tc)
        xp_ref[pl.ds(t0 + pad, tc), :] = xr_ref[pl.ds(t0, tc), :]
        return carry

    lax.fori_loop(0, n_chunks, copy_body, 0)

    half_log2_decay = (-0.5 * LRU_C * LOG2E) * jax.nn.softplus(-lam_ref[0])
    cw = cw_ref[...]
    cb = cb_ref[...]
    gb = gb_ref[0]
    bw = LRU_BLOCK_DIM

    def gate_body(c, carry):
        t0 = pl.multiple_of(c * tc, tc)
        xc = xp_ref[pl.ds(t0 + pad - 2, tc), :] * cw[0:1] + cb
        for kk in range(1, LRU_CONV):
            xc = xc + xp_ref[pl.ds(t0 + pad - 2 + kk, tc), :] * cw[kk:kk + 1]
        t = jnp.tanh(_dot(xc.astype(BF16), w_ref[0]) + gb)
        xh = 0.5 * xc
        for d, (a_ref, b_ref) in enumerate(((af_ref, bf_ref), (ab_ref, bb_ref))):
            log2_a = (1.0 + t[:, d * bw:(d + 1) * bw]) * half_log2_decay[d:d + 1]
            a = jnp.exp2(log2_a)
            a_ref[pl.ds(t0, tc), :] = a
            y = _one_minus_exp(log2_a * (2.0 * LN2), a * a)
            root = y * lax.rsqrt(jnp.maximum(y, F32_TINY))
            b_ref[pl.ds(t0, tc), :] = root * (1.0 + t[:, (2 + d) * bw:(3 + d) * bw]) * xh
        return carry

    lax.fori_loop(0, n_chunks, gate_body, 0)

    def seg(i):
        return pl.ds(i, SUBLANES, stride=pitch)

    def scan_steps(a_ref, b_ref, steps, h, p):
        ab = [(a_ref[seg(i), :], b_ref[seg(i), :]) for i in steps]
        for g0 in range(0, len(steps), LRU_LOOKAHEAD):
            a_run = b_run = None
            for i, (a, b) in zip(steps[g0:g0 + LRU_LOOKAHEAD], ab[g0:g0 + LRU_LOOKAHEAD]):
                a_run, b_run = (a, b) if a_run is None else (a * a_run, a * b_run + b)
                h_i = a_run * h + b_run
                p_i = a_run * p
                b_ref[seg(i), :] = h_i
                a_ref[seg(i), :] = p_i
            h, p = h_i, p_i
        return h, p

    def scan_body(it, carry):
        hf, pf, hb, pb = carry
        for k0 in range(0, LRU_SCAN_UNROLL, LRU_SCAN_BATCH):
            i0 = it * LRU_SCAN_UNROLL + k0
            hf, pf = scan_steps(af_ref, bf_ref, [i0 + k for k in range(LRU_SCAN_BATCH)], hf, pf)
            hb, pb = scan_steps(ab_ref, bb_ref, [pitch - 1 - i0 - k for k in range(LRU_SCAN_BATCH)], hb, pb)
        return hf, pf, hb, pb

    zero = jnp.zeros((SUBLANES, LANES), F32)
    one = jnp.ones((SUBLANES, LANES), F32)
    hf, pf, hb, pb = lax.fori_loop(0, pitch // LRU_SCAN_UNROLL, scan_body, (zero, one, zero, one))

    row = lax.broadcasted_iota(jnp.int32, (SUBLANES, LANES), 0)
    cf = zero
    c = jnp.zeros((1, LANES), F32)
    for s in range(1, SUBLANES):
        c = pf[s - 1:s] * c + hf[s - 1:s]
        cf = jnp.where(row == s, c, cf)
    cbk = zero
    c = jnp.zeros((1, LANES), F32)
    for s in range(SUBLANES - 2, -1, -1):
        c = pb[s + 1:s + 2] * c + hb[s + 1:s + 2]
        cbk = jnp.where(row == s, c, cbk)
    carry_ref[0:SUBLANES, :] = cf
    carry_ref[SUBLANES:2 * SUBLANES, :] = cbk

    to = min(LRU_CHUNK, seq // SUBLANES)
    rows = lax.broadcasted_iota(jnp.int32, (to, LANES), 0)

    def out_body(c, carry):
        t0 = pl.multiple_of(c * to, to)
        s0 = lax.div(t0, pitch)
        s1 = jnp.minimum(s0 + 1, SUBLANES - 1)
        in_first = rows < (s0 + 1) * pitch - t0
        cf_rows = jnp.where(in_first, carry_ref[pl.ds(s0, 1), :], carry_ref[pl.ds(s1, 1), :])
        cb_rows = jnp.where(in_first, carry_ref[pl.ds(SUBLANES + s0, 1), :],
                            carry_ref[pl.ds(SUBLANES + s1, 1), :])
        sl = pl.ds(t0, to)
        h = bf_ref[sl, :] + af_ref[sl, :] * cf_rows + bb_ref[sl, :] + ab_ref[sl, :] * cb_rows
        o_ref[sl, :] = h * _gelu_tanh(yg_ref[sl, :])
        return carry

    lax.fori_loop(0, seq // to, out_body, 0)


def _lru(xr, yg, conv_w, conv_b, w_cat, b_cat, lam):
    _, b, seq, bw = xr.shape
    lw = LRU_BLOCKS * bw
    pitch = seq // SUBLANES
    while pitch % (2 * LRU_SCAN_UNROLL) != LRU_SCAN_UNROLL:
        pitch += 1
    seq_blk = pl.BlockSpec((None, None, seq, bw), lambda bi, n: (n, bi, 0, 0))
    scan_rows = SUBLANES * pitch
    return pl.pallas_call(
        functools.partial(_lru_kernel, seq=seq, pitch=pitch),
        grid=(b, LRU_BLOCKS),
        in_specs=[
            seq_blk,
            seq_blk,
            pl.BlockSpec((LRU_CONV, bw), lambda bi, n: (0, n)),
            pl.BlockSpec((1, bw), lambda bi, n: (0, n)),
            pl.BlockSpec((1, bw, 4 * bw), lambda bi, n: (n, 0, 0)),
            pl.BlockSpec((1, 1, 4 * bw), lambda bi, n: (n, 0, 0)),
            pl.BlockSpec((1, 2, bw), lambda bi, n: (n, 0, 0)),
        ],
        out_specs=seq_blk,
        out_shape=jax.ShapeDtypeStruct((LRU_BLOCKS, b, seq, bw), F32),
        scratch_shapes=[
            pltpu.VMEM((seq + 2 * SUBLANES, LANES), F32),
            pltpu.VMEM((scan_rows, LANES), F32),
            pltpu.VMEM((scan_rows, LANES), F32),
            pltpu.VMEM((scan_rows, LANES), F32),
            pltpu.VMEM((scan_rows, LANES), F32),
            pltpu.VMEM((2 * SUBLANES, LANES), F32),
        ],
        compiler_params=_cparams(("parallel", "parallel"), 52),
        name="rglru",
    )(xr, yg, conv_w, conv_b.reshape(1, lw), w_cat, b_cat, lam)


def _out_kernel(x_ref, attn_ref, lru_ref, mod_ref, ag_ref, lg_ref, w_ref, o_ref):
    attn = jnp.concatenate([attn_ref[c] for c in range(attn_ref.shape[0])], axis=1)
    lru = jnp.concatenate([lru_ref[c] for c in range(lru_ref.shape[0])], axis=1)
    aw = attn.shape[1]
    na = _rms(attn, ag_ref[...]).astype(BF16)
    nl = _rms(lru, lg_ref[...]).astype(BF16)
    y = _dot(na, w_ref[0:aw, :]) + _dot(nl, w_ref[aw:, :])
    o_ref[...] = x_ref[...] + mod_ref[0][2:3] * y


def _out_proj(x2, attn3, lru3, mod6, row0, seq, attn_g, lru_g, w_out, layer):
    t, d = x2.shape
    aw = attn3.shape[0] * attn3.shape[2]
    lw = lru3.shape[0] * lru3.shape[2]
    tm = TOKEN_TILE
    tiles_per_seq = seq // tm
    tok = lambda w: pl.BlockSpec((tm, w), lambda i: (i, 0))
    blocked = lambda a: pl.BlockSpec((a.shape[0], tm, a.shape[2]), lambda i: (0, i, 0))
    const = lambda shape: pl.BlockSpec(shape, lambda i: (0,) * len(shape))
    return pl.pallas_call(
        _out_kernel,
        grid=(t // tm,),
        in_specs=[
            tok(d), blocked(attn3), blocked(lru3),
            pl.BlockSpec((1, 6, d), lambda i: (row0 + i // tiles_per_seq, 0, 0)),
            const((1, aw)), const((1, lw)),
            pl.BlockSpec((None,) + w_out.shape[1:], lambda i: (layer, 0, 0)),
        ],
        out_specs=tok(d),
        out_shape=jax.ShapeDtypeStruct((t, d), F32),
        compiler_params=_cparams(("parallel",), 48),
        name="out_proj",
    )(x2, attn3, lru3, mod6, attn_g.reshape(1, aw), lru_g.reshape(1, lw), w_out)


def _ffn_kernel(x_ref, xp_ref, xn_ref, mod_ref, g_ref, wg_ref, wu_ref, cw_ref, cb_ref, wd_ref, o_ref,
                h_ref, ge_ref, *, tiles_per_seq):
    i = pl.program_id(0)
    j = pl.program_id(1)
    tm = x_ref.shape[0]

    @pl.when(j == 0)
    def _():
        mod = mod_ref[0]

        def norm_mod(x):
            return _rms(x, g_ref[...]) * (1.0 + mod[4:5]) + mod[3:4]

        first = (i % tiles_per_seq) == 0
        last = (i % tiles_per_seq) == tiles_per_seq - 1
        h_ref[0:HALO, :] = jnp.where(first, 0.0, norm_mod(xp_ref[...])).astype(BF16)
        h_ref[HALO:HALO + tm, :] = norm_mod(x_ref[...]).astype(BF16)
        h_ref[HALO + tm:, :] = jnp.where(last, 0.0, norm_mod(xn_ref[...])).astype(BF16)
        o_ref[...] = jnp.zeros_like(o_ref)

    ge_ref[...] = _dot(h_ref[...], wg_ref[...])
    u = _dot(h_ref[HALO:HALO + tm, :], wu_ref[...])
    cw = cw_ref[...]
    g = ge_ref[HALO - 1:HALO - 1 + tm, :] * cw[0:1] + cb_ref[...]
    g = g + ge_ref[HALO:HALO + tm, :] * cw[1:2]
    g = g + ge_ref[HALO + 1:HALO + 1 + tm, :] * cw[2:3]
    act = (_gelu_tanh(g) * u).astype(BF16)
    o_ref[...] += _dot(act, wd_ref[...])

    @pl.when(j == pl.num_programs(1) - 1)
    def _():
        o_ref[...] = x_ref[...] + mod_ref[0][5:6] * o_ref[...]


def _ffn(x2, mod6, row0, seq, norm_g, w_gate, w_up, conv_w, conv_b, w_down, layer):
    t, d = x2.shape
    f = w_gate.shape[2]
    fc = FF_CHUNK
    tm = min(FFN_TOKEN_TILE, seq)
    tiles_per_seq = seq // tm
    hb = tm // HALO
    n_halo_blocks = t // HALO
    return pl.pallas_call(
        functools.partial(_ffn_kernel, tiles_per_seq=tiles_per_seq),
        grid=(t // tm, f // fc),
        in_specs=[
            pl.BlockSpec((tm, d), lambda i, j: (i, 0)),
            pl.BlockSpec((HALO, d), lambda i, j: (jnp.maximum(i * hb - 1, 0), 0)),
            pl.BlockSpec((HALO, d), lambda i, j: (jnp.minimum((i + 1) * hb, n_halo_blocks - 1), 0)),
            pl.BlockSpec((1, 6, d), lambda i, j: (row0 + i // tiles_per_seq, 0, 0)),
            pl.BlockSpec((1, d), lambda i, j: (0, 0)),
            pl.BlockSpec((None, d, fc), lambda i, j: (layer, 0, j)),
            pl.BlockSpec((None, d, fc), lambda i, j: (layer, 0, j)),
            pl.BlockSpec((FFN_CONV, fc), lambda i, j: (0, j)),
            pl.BlockSpec((1, fc), lambda i, j: (0, j)),
            pl.BlockSpec((None, fc, d), lambda i, j: (layer, j, 0)),
        ],
        out_specs=pl.BlockSpec((tm, d), lambda i, j: (i, 0)),
        out_shape=jax.ShapeDtypeStruct((t, d), F32),
        scratch_shapes=[pltpu.VMEM((tm + 2 * HALO, d), BF16), pltpu.VMEM((tm + 2 * HALO, fc), F32)],
        compiler_params=_cparams(("parallel", "arbitrary"), 57),
        name="convglu_ffn",
    )(x2, x2, x2, mod6, norm_g.reshape(1, d), w_gate, w_up, conv_w, conv_b.reshape(1, f), w_down)


def kernel(x_prompt, x_sample, c_prompt, c_sample, rel_bias, w_mod, b_mod, norm1_g, norm2_g, w_in, q_norm_g, k_norm_g, attn_sink, lru_conv_w, lru_conv_b, lru_w_a, lru_b_a, lru_w_x, lru_b_x, lru_lambda, attn_out_g, lru_out_g, w_out, ffn_w_gate, ffn_w_up, ffn_conv_w, ffn_conv_b, ffn_w_down):
    depth, d, _ = w_in.shape
    n_prompt = c_prompt.shape[0]
    n_req = n_prompt + c_sample.shape[0]
    rows = -(-n_req // SUBLANES) * SUBLANES
    c_all = jnp.concatenate([c_prompt, c_sample, jnp.zeros((rows - n_req, d), F32)], axis=0)
    mod = _modulation(c_all, w_mod, b_mod).reshape(depth, rows, 6, d)
    bias = _bias_table(rel_bias)

    w_in_b = w_in.astype(BF16)
    w_out_b = w_out.astype(BF16)
    w_gate_b = ffn_w_gate.astype(BF16)
    w_up_b = ffn_w_up.astype(BF16)
    w_down_b = ffn_w_down.astype(BF16)
    w_cat = jnp.concatenate([lru_w_a[:, 0], lru_w_a[:, 1], lru_w_x[:, 0], lru_w_x[:, 1]], axis=-1)
    w_cat = (0.5 * w_cat).astype(BF16)
    bw = LRU_BLOCK_DIM
    b_cat = jnp.concatenate(
        [lru_b_a.reshape(depth, 2, LRU_BLOCKS, bw), lru_b_x.reshape(depth, 2, LRU_BLOCKS, bw)], axis=1)
    b_cat = 0.5 * jnp.transpose(b_cat, (0, 2, 1, 3)).reshape(depth, LRU_BLOCKS, 1, 4 * bw)
    lam = jnp.transpose(lru_lambda.reshape(depth, 2, LRU_BLOCKS, bw), (0, 2, 1, 3))
    sink_rows = jnp.repeat(attn_sink.reshape(depth, N_KV_HEADS, GROUP), WINDOW, axis=-1)[..., None]
    sink_rows = jnp.broadcast_to(sink_rows, (depth, N_KV_HEADS, GROUP * WINDOW, HEAD_DIM))

    def run(x, row0):
        b, seq, _ = x.shape
        x2 = x.reshape(b * seq, d)
        for l in range(depth):
            q, k, v, xr, yg = _in_proj(x2, mod[l], row0, seq, norm1_g[l], w_in_b, l, q_norm_g[l], k_norm_g[l])
            sh = lambda a: a.reshape(a.shape[0], b, seq, a.shape[-1])
            flat = lambda a: a.reshape(a.shape[0], b * seq, a.shape[-1])
            attn = _attention(sh(q), sh(k), sh(v), bias, sink_rows[l])
            lru = _lru(sh(xr), sh(yg), lru_conv_w[l], lru_conv_b[l], w_cat[l], b_cat[l], lam[l])
            x2 = _out_proj(x2, flat(attn), flat(lru), mod[l], row0, seq,
                           attn_out_g[l], lru_out_g[l], w_out_b, l)
            x2 = _ffn(x2, mod[l], row0, seq, norm2_g[l], w_gate_b, w_up_b, ffn_conv_w[l], ffn_conv_b[l],
                      w_down_b, l)
        return x2.reshape(b, seq, d)

    return (run(x_prompt, 0), run(x_sample, n_prompt))
```

```python
import functools
import math

import jax
import jax.numpy as jnp
from jax import lax
from jax.experimental import pallas as pl
from jax.experimental.pallas import tpu as pltpu

F32 = jnp.float32
BF16 = jnp.bfloat16

EPS = 1e-6
HEAD_DIM = 128
N_HEADS = 8
N_KV_HEADS = 2
GROUP = N_HEADS // N_KV_HEADS
WINDOW = 128
N_BUCKETS = 32
LRU_BLOCKS = 8
LRU_BLOCK_DIM = 128
LRU_CONV = 4
LRU_C = 8.0
FFN_CONV = 3
NEG_INF = -1e30
LOG2E = math.log2(math.e)
LN2 = math.log(2.0)
SERIES_LIMIT = 0.02
F32_TINY = float(jnp.finfo(jnp.float32).tiny)

SUBLANES = 8
LANES = 128
MIB = 1024 * 1024

TOKEN_TILE = 512
FFN_TOKEN_TILE = 512
FF_CHUNK = 1024
MOD_CHUNK = 1024
ATTN_Q_TILE = 1024
LRU_CHUNK = 512
LRU_SCAN_UNROLL = 12
LRU_SCAN_BATCH = 6
LRU_LOOKAHEAD = 3
ATTN_UNROLL = 4
HALO = SUBLANES


def _cparams(semantics, vmem_mib):
    return pltpu.CompilerParams(dimension_semantics=semantics, vmem_limit_bytes=vmem_mib * MIB)


def _dot(a, b):
    return jnp.dot(a, b, preferred_element_type=F32)


def _rms(x, gain):
    return x * lax.rsqrt(jnp.mean(x * x, axis=-1, keepdims=True) + EPS) * gain


def _gelu_tanh(x):
    return 0.5 * x * (1.0 + jnp.tanh(math.sqrt(2.0 / math.pi) * (x + 0.044715 * (x * x * x))))


def _sigmoid(x):
    return 1.0 / (1.0 + jnp.exp(-x))


def _one_minus_exp(x, exp_x):
    series = x * (-1.0 + x * (-1 / 2 + x * (-1 / 6 + x * (-1 / 24))))
    return jnp.where(x > -SERIES_LIMIT, series, 1.0 - exp_x)


def _mod_kernel(c_ref, w_ref, b_ref, o_ref):
    c = c_ref[...]
    cs = (c * _sigmoid(c)).astype(BF16)
    o_ref[0] = _dot(cs, w_ref[0].astype(BF16)) + b_ref[0]


def _modulation(c_all, w_mod, b_mod):
    depth, d, n = w_mod.shape
    rows = c_all.shape[0]
    return pl.pallas_call(
        _mod_kernel,
        grid=(depth, n // MOD_CHUNK),
        in_specs=[
            pl.BlockSpec((rows, d), lambda l, j: (0, 0)),
            pl.BlockSpec((1, d, MOD_CHUNK), lambda l, j: (l, 0, j)),
            pl.BlockSpec((1, 1, MOD_CHUNK), lambda l, j: (l, 0, j)),
        ],
        out_specs=pl.BlockSpec((1, rows, MOD_CHUNK), lambda l, j: (l, 0, j)),
        out_shape=jax.ShapeDtypeStruct((depth, rows, n), F32),
        compiler_params=_cparams(("arbitrary", "arbitrary"), 40),
        name="adaln_modulation",
    )(c_all, w_mod, b_mod.reshape(depth, 1, n))


_BUCKET_THRESHOLDS = (12, 16, 23, 32, 46, 64, 91)


def _bias_kernel(rb_ref, o_ref):
    head = pl.program_id(0)
    qi = lax.broadcasted_iota(jnp.int32, (WINDOW, 3 * WINDOW), 0)
    kj = lax.broadcasted_iota(jnp.int32, (WINDOW, 3 * WINDOW), 1)
    rel = kj - WINDOW - qi
    n = jnp.abs(rel)
    half = N_BUCKETS // 2
    large = jnp.full_like(n, half // 2)
    for t in _BUCKET_THRESHOLDS:
        large = large + jnp.where(n >= t, 1, 0)
    bucket = jnp.where(rel > 0, half, 0) + jnp.where(n < half // 2, n, large)
    bias = jnp.zeros((WINDOW, 3 * WINDOW), F32)
    for b in range(N_BUCKETS):
        bias = jnp.where(bucket == b, rb_ref[b, head], bias)
    o_ref[0] = jnp.where(n <= WINDOW, bias * LOG2E, NEG_INF)


def _bias_table(rel_bias):
    out = pl.pallas_call(
        _bias_kernel,
        grid=(N_HEADS,),
        in_specs=[pl.BlockSpec(memory_space=pltpu.SMEM)],
        out_specs=pl.BlockSpec((1, WINDOW, 3 * WINDOW), lambda h: (h, 0, 0)),
        out_shape=jax.ShapeDtypeStruct((N_HEADS, WINDOW, 3 * WINDOW), F32),
        compiler_params=_cparams(("arbitrary",), 16),
        name="rel_bias_table",
    )(rel_bias)
    return out.reshape(N_KV_HEADS, GROUP * WINDOW, 3 * WINDOW)


def _in_kernel(x_ref, mod_ref, g_ref, w_ref, qg_ref, kg_ref, q_ref, k_ref, v_ref, xr_ref, yg_ref):
    mod = mod_ref[0]
    h = _rms(x_ref[...], g_ref[...]) * (1.0 + mod[1:2]) + mod[0:1]
    hb = h.astype(BF16)
    nc = GROUP * HEAD_DIM
    for c in range(N_KV_HEADS):
        z = _dot(hb, w_ref[:, c * nc:(c + 1) * nc])
        for i in range(GROUP):
            zh = z[:, i * HEAD_DIM:(i + 1) * HEAD_DIM]
            q_ref[c, :, i * HEAD_DIM:(i + 1) * HEAD_DIM] = _rms(zh, qg_ref[...]).astype(BF16)
    aw = N_KV_HEADS * nc
    kvw = N_KV_HEADS * HEAD_DIM
    z = _dot(hb, w_ref[:, aw:aw + 2 * kvw])
    for i in range(N_KV_HEADS):
        k_ref[i] = _rms(z[:, i * HEAD_DIM:(i + 1) * HEAD_DIM], kg_ref[...]).astype(BF16)
        v_ref[i] = z[:, kvw + i * HEAD_DIM:kvw + (i + 1) * HEAD_DIM].astype(BF16)
    base = aw + 2 * kvw
    per = nc // LRU_BLOCK_DIM
    for dst_ref, act in ((xr_ref, None), (yg_ref, _gelu_tanh)):
        for c in range(LRU_BLOCKS // per):
            z = _dot(hb, w_ref[:, base + c * nc:base + (c + 1) * nc])
            z = z if act is None else act(z)
            for i in range(per):
                dst_ref[c * per + i] = z[:, i * LRU_BLOCK_DIM:(i + 1) * LRU_BLOCK_DIM]
        base = base + LRU_BLOCKS * LRU_BLOCK_DIM


def _in_proj(x2, mod6, row0, seq, norm_g, w_in, layer, q_g, k_g):
    t, d = x2.shape
    gw = GROUP * HEAD_DIM
    tm = TOKEN_TILE
    tiles_per_seq = seq // tm
    blocked = lambda n, w: pl.BlockSpec((n, tm, w), lambda i: (0, i, 0))
    const = lambda shape: pl.BlockSpec(shape, lambda i: (0,) * len(shape))
    return pl.pallas_call(
        _in_kernel,
        grid=(t // tm,),
        in_specs=[
            pl.BlockSpec((tm, d), lambda i: (i, 0)),
            pl.BlockSpec((1, 6, d), lambda i: (row0 + i // tiles_per_seq, 0, 0)),
            const((1, d)),
            pl.BlockSpec((None,) + w_in.shape[1:], lambda i: (layer, 0, 0)),
            const((1, HEAD_DIM)),
            const((1, HEAD_DIM)),
        ],
        out_specs=[blocked(N_KV_HEADS, gw), blocked(N_KV_HEADS, HEAD_DIM), blocked(N_KV_HEADS, HEAD_DIM),
                   blocked(LRU_BLOCKS, LRU_BLOCK_DIM), blocked(LRU_BLOCKS, LRU_BLOCK_DIM)],
        out_shape=[
            jax.ShapeDtypeStruct((N_KV_HEADS, t, gw), BF16),
            jax.ShapeDtypeStruct((N_KV_HEADS, t, HEAD_DIM), BF16),
            jax.ShapeDtypeStruct((N_KV_HEADS, t, HEAD_DIM), BF16),
            jax.ShapeDtypeStruct((LRU_BLOCKS, t, LRU_BLOCK_DIM), F32),
            jax.ShapeDtypeStruct((LRU_BLOCKS, t, LRU_BLOCK_DIM), F32),
        ],
        compiler_params=_cparams(("parallel",), 52),
        name="in_proj",
    )(x2, mod6, norm_g.reshape(1, d), w_in, q_g.reshape(1, HEAD_DIM), k_g.reshape(1, HEAD_DIM))


def _attn_kernel(q_ref, k_ref, v_ref, bias_ref, sink_ref, o_ref, *, seq):
    nb = seq // WINDOW
    blocks = q_ref.shape[0] // WINDOW
    chunk = pl.program_id(2)
    scale = HEAD_DIM ** -0.5 * LOG2E
    sink = sink_ref[0] * LOG2E
    col = lax.broadcasted_iota(jnp.int32, (1, 3 * WINDOW), 1)
    ones = jnp.ones((3 * WINDOW, HEAD_DIM), BF16)
    rows = GROUP * WINDOW

    def scores(j):
        n = chunk * blocks + j
        r0 = pl.multiple_of(j * WINDOW, WINDOW)
        qs = jnp.concatenate(
            [q_ref[pl.ds(r0, WINDOW), g * HEAD_DIM:(g + 1) * HEAD_DIM] for g in range(GROUP)], axis=0)
        starts = [pl.multiple_of(jnp.maximum(n - 1, 0) * WINDOW, WINDOW),
                  pl.multiple_of(n * WINDOW, WINDOW),
                  pl.multiple_of(jnp.minimum(n + 1, nb - 1) * WINDOW, WINDOW)]
        kw = jnp.concatenate([k_ref[pl.ds(s, WINDOW), :] for s in starts], axis=0)
        vw = jnp.concatenate(
            [jnp.concatenate([v_ref[pl.ds(s, WINDOW), :] for s in starts], axis=0), ones], axis=1)
        s = lax.dot_general(qs, kw, (((1,), (1,)), ((), ())), preferred_element_type=F32)
        return n, r0, s, vw

    def softmax(n, r0, s, vw):
        s = s * scale + bias_ref[0]
        valid = ((col >= WINDOW) | (n > 0)) & ((col < 2 * WINDOW) | (n < nb - 1))
        s = jnp.where(valid, s, NEG_INF)
        m = jnp.maximum(jnp.broadcast_to(jnp.max(s, axis=-1, keepdims=True), (rows, HEAD_DIM)), sink)
        p = jnp.exp2(s - jnp.concatenate([m, m, m], axis=1)).astype(BF16)
        return r0, p, jnp.exp2(sink - m), vw

    def finish(r0, p, sink_term, vw):
        ov = _dot(p, vw)
        o = ov[:, :HEAD_DIM] / (ov[:, HEAD_DIM:] + sink_term)
        for g in range(GROUP):
            o_ref[pl.ds(r0, WINDOW), g * HEAD_DIM:(g + 1) * HEAD_DIM] = o[g * WINDOW:(g + 1) * WINDOW]

    def body(jj, carry):
        stage = [scores(jj * ATTN_UNROLL + u) for u in range(ATTN_UNROLL)]
        stage = [softmax(*st) for st in stage]
        for st in stage:
            finish(*st)
        return carry

    lax.fori_loop(0, blocks // ATTN_UNROLL, body, 0)


def _attention(q, k, v, bias, sink_rows):
    _, b, seq, gw = q.shape
    tq = min(ATTN_Q_TILE, seq)
    seq_blk = pl.BlockSpec((None, None, seq, HEAD_DIM), lambda bi, h, c: (h, bi, 0, 0))
    return pl.pallas_call(
        functools.partial(_attn_kernel, seq=seq),
        grid=(b, N_KV_HEADS, seq // tq),
        in_specs=[
            pl.BlockSpec((None, None, tq, gw), lambda bi, h, c: (h, bi, c, 0)),
            seq_blk,
            seq_blk,
            pl.BlockSpec((1, GROUP * WINDOW, 3 * WINDOW), lambda bi, h, c: (h, 0, 0)),
            pl.BlockSpec((1, GROUP * WINDOW, HEAD_DIM), lambda bi, h, c: (h, 0, 0)),
        ],
        out_specs=pl.BlockSpec((None, None, tq, gw), lambda bi, h, c: (h, bi, c, 0)),
        out_shape=jax.ShapeDtypeStruct((N_KV_HEADS, b, seq, gw), F32),
        compiler_params=_cparams(("parallel", "parallel", "arbitrary"), 40),
        name="window_attention",
    )(q, k, v, bias, sink_rows)


def _lru_kernel(xr_ref, yg_ref, cw_ref, cb_ref, w_ref, gb_ref, lam_ref, o_ref,
                xp_ref, af_ref, bf_ref, ab_ref, bb_ref, carry_ref, *, seq, pitch):
    tc = min(LRU_CHUNK, seq)
    n_chunks = seq // tc
    pad = SUBLANES
    zeros8 = jnp.zeros((pad, LANES), F32)
    xp_ref[0:pad, :] = zeros8
    xp_ref[seq + pad:seq + 2 * pad, :] = zeros8
    tail = SUBLANES * pitch - seq
    for r in (af_ref, bf_ref, ab_ref, bb_ref):
        r[seq:seq + tail, :] = jnp.zeros((tail, LANES), F32)

    def copy_body(c, carry):
        t0 = pl.multiple_of(c * tc, tc)
        xp_ref[pl.ds(t0 + pad, tc), :] = xr_ref[pl.ds(t0, tc), :]
        return carry

    lax.fori_loop(0, n_chunks, copy_body, 0)

    half_log2_decay = (-0.5 * LRU_C * LOG2E) * jax.nn.softplus(-lam_ref[0])
    cw = cw_ref[...]
    cb = cb_ref[...]
    gb = gb_ref[0]
    bw = LRU_BLOCK_DIM

    def gate_body(c, carry):
        t0 = pl.multiple_of(c * tc, tc)
        xc = xp_ref[pl.ds(t0 + pad - 2, tc), :] * cw[0:1] + cb
        for kk in range(1, LRU_CONV):
            xc = xc + xp_ref[pl.ds(t0 + pad - 2 + kk, tc), :] * cw[kk:kk + 1]
        t = jnp.tanh(_dot(xc.astype(BF16), w_ref[0]) + gb)
        xh = 0.5 * xc
        for d, (a_ref, b_ref) in enumerate(((af_ref, bf_ref), (ab_ref, bb_ref))):
            log2_a = (1.0 + t[:, d * bw:(d + 1) * bw]) * half_log2_decay[d:d + 1]
            a = jnp.exp2(log2_a)
            a_ref[pl.ds(t0, tc), :] = a
            y = _one_minus_exp(log2_a * (2.0 * LN2), a * a)
            root = y * lax.rsqrt(jnp.maximum(y, F32_TINY))
            b_ref[pl.ds(t0, tc), :] = root * (1.0 + t[:, (2 + d) * bw:(3 + d) * bw]) * xh
        return carry

    lax.fori_loop(0, n_chunks, gate_body, 0)

    def seg(i):
        return pl.ds(i, SUBLANES, stride=pitch)

    def scan_steps(a_ref, b_ref, steps, h, p):
        ab = [(a_ref[seg(i), :], b_ref[seg(i), :]) for i in steps]
        for g0 in range(0, len(steps), LRU_LOOKAHEAD):
            a_run = b_run = None
            for i, (a, b) in zip(steps[g0:g0 + LRU_LOOKAHEAD], ab[g0:g0 + LRU_LOOKAHEAD]):
                a_run, b_run = (a, b) if a_run is None else (a * a_run, a * b_run + b)
                h_i = a_run * h + b_run
                p_i = a_run * p
                b_ref[seg(i), :] = h_i
                a_ref[seg(i), :] = p_i
            h, p = h_i, p_i
        return h, p

    def scan_body(it, carry):
        hf, pf, hb, pb = carry
        for k0 in range(0, LRU_SCAN_UNROLL, LRU_SCAN_BATCH):
            i0 = it * LRU_SCAN_UNROLL + k0
            hf, pf = scan_steps(af_ref, bf_ref, [i0 + k for k in range(LRU_SCAN_BATCH)], hf, pf)
            hb, pb = scan_steps(ab_ref, bb_ref, [pitch - 1 - i0 - k for k in range(LRU_SCAN_BATCH)], hb, pb)
        return hf, pf, hb, pb

    zero = jnp.zeros((SUBLANES, LANES), F32)
    one = jnp.ones((SUBLANES, LANES), F32)
    hf, pf, hb, pb = lax.fori_loop(0, pitch // LRU_SCAN_UNROLL, scan_body, (zero, one, zero, one))

    row = lax.broadcasted_iota(jnp.int32, (SUBLANES, LANES), 0)
    cf = zero
    c = jnp.zeros((1, LANES), F32)
    for s in range(1, SUBLANES):
        c = pf[s - 1:s] * c + hf[s - 1:s]
        cf = jnp.where(row == s, c, cf)
    cbk = zero
    c = jnp.zeros((1, LANES), F32)
    for s in range(SUBLANES - 2, -1, -1):
        c = pb[s + 1:s + 2] * c + hb[s + 1:s + 2]
        cbk = jnp.where(row == s, c, cbk)
    carry_ref[0:SUBLANES, :] = cf
    carry_ref[SUBLANES:2 * SUBLANES, :] = cbk

    to = min(LRU_CHUNK, seq // SUBLANES)
    rows = lax.broadcasted_iota(jnp.int32, (to, LANES), 0)

    def out_body(c, carry):
        t0 = pl.multiple_of(c * to, to)
        s0 = lax.div(t0, pitch)
        s1 = jnp.minimum(s0 + 1, SUBLANES - 1)
        in_first = rows < (s0 + 1) * pitch - t0
        cf_rows = jnp.where(in_first, carry_ref[pl.ds(s0, 1), :], carry_ref[pl.ds(s1, 1), :])
        cb_rows = jnp.where(in_first, carry_ref[pl.ds(SUBLANES + s0, 1), :],
                            carry_ref[pl.ds(SUBLANES + s1, 1), :])
        sl = pl.ds(t0, to)
        h = bf_ref[sl, :] + af_ref[sl, :] * cf_rows + bb_ref[sl, :] + ab_ref[sl, :] * cb_rows
        o_ref[sl, :] = h * yg_ref[sl, :]
        return carry

    lax.fori_loop(0, seq // to, out_body, 0)


def _lru(xr, yg, conv_w, conv_b, w_cat, b_cat, lam):
    _, b, seq, bw = xr.shape
    lw = LRU_BLOCKS * bw
    pitch = seq // SUBLANES
    while pitch % (2 * LRU_SCAN_UNROLL) != LRU_SCAN_UNROLL:
        pitch += 1
    seq_blk = pl.BlockSpec((None, None, seq, bw), lambda bi, n: (n, bi, 0, 0))
    scan_rows = SUBLANES * pitch
    return pl.pallas_call(
        functools.partial(_lru_kernel, seq=seq, pitch=pitch),
        grid=(b, LRU_BLOCKS),
        in_specs=[
            seq_blk,
            seq_blk,
            pl.BlockSpec((LRU_CONV, bw), lambda bi, n: (0, n)),
            pl.BlockSpec((1, bw), lambda bi, n: (0, n)),
            pl.BlockSpec((1, bw, 4 * bw), lambda bi, n: (n, 0, 0)),
            pl.BlockSpec((1, 1, 4 * bw), lambda bi, n: (n, 0, 0)),
            pl.BlockSpec((1, 2, bw), lambda bi, n: (n, 0, 0)),
        ],
        out_specs=seq_blk,
        out_shape=jax.ShapeDtypeStruct((LRU_BLOCKS, b, seq, bw), F32),
        scratch_shapes=[
            pltpu.VMEM((seq + 2 * SUBLANES, LANES), F32),
            pltpu.VMEM((scan_rows, LANES), F32),
            pltpu.VMEM((scan_rows, LANES), F32),
            pltpu.VMEM((scan_rows, LANES), F32),
            pltpu.VMEM((scan_rows, LANES), F32),
            pltpu.VMEM((2 * SUBLANES, LANES), F32),
        ],
        compiler_params=_cparams(("parallel", "parallel"), 52),
        name="rglru",
    )(xr, yg, conv_w, conv_b.reshape(1, lw), w_cat, b_cat, lam)


def _out_kernel(x_ref, attn_ref, lru_ref, mod_ref, ag_ref, lg_ref, w_ref, n2_ref, o_ref, h_ref):
    attn = jnp.concatenate([attn_ref[c] for c in range(attn_ref.shape[0])], axis=1)
    lru = jnp.concatenate([lru_ref[c] for c in range(lru_ref.shape[0])], axis=1)
    aw = attn.shape[1]
    na = _rms(attn, ag_ref[...]).astype(BF16)
    nl = _rms(lru, lg_ref[...]).astype(BF16)
    y = _dot(na, w_ref[0:aw, :]) + _dot(nl, w_ref[aw:, :])
    mod = mod_ref[0]
    x1 = x_ref[...] + mod[2:3] * y
    o_ref[...] = x1
    h_ref[...] = (_rms(x1, n2_ref[...]) * (1.0 + mod[4:5]) + mod[3:4]).astype(BF16)


def _out_proj(x2, attn3, lru3, mod6, row0, seq, attn_g, lru_g, w_out, layer, norm2_g):
    t, d = x2.shape
    aw = attn3.shape[0] * attn3.shape[2]
    lw = lru3.shape[0] * lru3.shape[2]
    tm = TOKEN_TILE
    tiles_per_seq = seq // tm
    tok = lambda w: pl.BlockSpec((tm, w), lambda i: (i, 0))
    blocked = lambda a: pl.BlockSpec((a.shape[0], tm, a.shape[2]), lambda i: (0, i, 0))
    const = lambda shape: pl.BlockSpec(shape, lambda i: (0,) * len(shape))
    return pl.pallas_call(
        _out_kernel,
        grid=(t // tm,),
        in_specs=[
            tok(d), blocked(attn3), blocked(lru3),
            pl.BlockSpec((1, 6, d), lambda i: (row0 + i // tiles_per_seq, 0, 0)),
            const((1, aw)), const((1, lw)),
            pl.BlockSpec((None,) + w_out.shape[1:], lambda i: (layer, 0, 0)),
            const((1, d)),
        ],
        out_specs=[tok(d), tok(d)],
        out_shape=[jax.ShapeDtypeStruct((t, d), F32), jax.ShapeDtypeStruct((t, d), BF16)],
        compiler_params=_cparams(("parallel",), 52),
        name="out_proj",
    )(x2, attn3, lru3, mod6, attn_g.reshape(1, aw), lru_g.reshape(1, lw), w_out, norm2_g.reshape(1, d))


def _halo_kernel(h_ref, w_ref, o_ref):
    o_ref[...] = _dot(h_ref[...], w_ref[...])


def _ffn_halo(h2, seq, tm, w_gate, layer):
    t, d = h2.shape
    f = w_gate.shape[2]
    n_tiles = t // tm
    tiles_per_seq = seq // tm
    tiles = h2.reshape(n_tiles, tm, d)
    zero_row = jnp.zeros((1, d), h2.dtype)
    pos = jnp.arange(n_tiles) % tiles_per_seq
    prev = jnp.concatenate([zero_row, tiles[:-1, tm - 1]], axis=0)
    prev = jnp.where((pos == 0)[:, None], 0, prev)
    nxt = jnp.concatenate([tiles[1:, 0], zero_row], axis=0)
    nxt = jnp.where((pos == tiles_per_seq - 1)[:, None], 0, nxt)
    rows = jnp.stack([prev, nxt], axis=1).reshape(2 * n_tiles, d)
    out = pl.pallas_call(
        _halo_kernel,
        grid=(f // FF_CHUNK,),
        in_specs=[
            pl.BlockSpec((2 * n_tiles, d), lambda j: (0, 0)),
            pl.BlockSpec((None, d, FF_CHUNK), lambda j: (layer, 0, j)),
        ],
        out_specs=pl.BlockSpec((2 * n_tiles, FF_CHUNK), lambda j: (0, j)),
        out_shape=jax.ShapeDtypeStruct((2 * n_tiles, f), F32),
        compiler_params=_cparams(("arbitrary",), 32),
        name="ffn_halo",
    )(rows, w_gate)
    return out.reshape(n_tiles, 2, f)


def _ffn_kernel(x_ref, h_ref, gh_ref, mod_ref, wg_ref, wu_ref, cw_ref, cb_ref, wd_ref, o_ref, ge_ref):
    j = pl.program_id(1)
    tm = x_ref.shape[0]
    fc = wg_ref.shape[1]

    @pl.when(j == 0)
    def _():
        o_ref[...] = jnp.zeros_like(o_ref)

    cols = pl.ds(pl.multiple_of(j * fc, fc), fc)
    h = h_ref[...]
    ge_ref[HALO:HALO + tm, :] = _dot(h, wg_ref[...])
    ge_ref[HALO - 1:HALO, :] = gh_ref[0, 0:1, cols]
    ge_ref[HALO + tm:HALO + tm + 1, :] = gh_ref[0, 1:2, cols]
    u = _dot(h, wu_ref[...])
    cw = cw_ref[:, cols]
    g = ge_ref[HALO - 1:HALO - 1 + tm, :] * cw[0:1] + cb_ref[:, cols]
    g = g + ge_ref[HALO:HALO + tm, :] * cw[1:2]
    g = g + ge_ref[HALO + 1:HALO + 1 + tm, :] * cw[2:3]
    act = (_gelu_tanh(g) * u).astype(BF16)
    o_ref[...] += _dot(act, wd_ref[...])

    @pl.when(j == pl.num_programs(1) - 1)
    def _():
        o_ref[...] = x_ref[...] + mod_ref[0][5:6] * o_ref[...]


def _ffn(x2, h2, mod6, row0, seq, w_gate, w_up, conv_w, conv_b, w_down, layer):
    t, d = x2.shape
    f = w_gate.shape[2]
    fc = FF_CHUNK
    tm = min(FFN_TOKEN_TILE, seq)
    tiles_per_seq = seq // tm
    ge_halo = _ffn_halo(h2, seq, tm, w_gate, layer)
    return pl.pallas_call(
        _ffn_kernel,
        grid=(t // tm, f // fc),
        in_specs=[
            pl.BlockSpec((tm, d), lambda i, j: (i, 0)),
            pl.BlockSpec((tm, d), lambda i, j: (i, 0)),
            pl.BlockSpec((1, 2, f), lambda i, j: (i, 0, 0)),
            pl.BlockSpec((1, 6, d), lambda i, j: (row0 + i // tiles_per_seq, 0, 0)),
            pl.BlockSpec((None, d, fc), lambda i, j: (layer, 0, j)),
            pl.BlockSpec((None, d, fc), lambda i, j: (layer, 0, j)),
            pl.BlockSpec((FFN_CONV, f), lambda i, j: (0, 0)),
            pl.BlockSpec((1, f), lambda i, j: (0, 0)),
            pl.BlockSpec((None, fc, d), lambda i, j: (layer, j, 0)),
        ],
        out_specs=pl.BlockSpec((tm, d), lambda i, j: (i, 0)),
        out_shape=jax.ShapeDtypeStruct((t, d), F32),
        scratch_shapes=[pltpu.VMEM((tm + 2 * HALO, fc), F32)],
        compiler_params=_cparams(("parallel", "arbitrary"), 57),
        name="convglu_ffn",
    )(x2, h2, ge_halo, mod6, w_gate, w_up, conv_w, conv_b.reshape(1, f), w_down)


def kernel(x_prompt, x_sample, c_prompt, c_sample, rel_bias, w_mod, b_mod, norm1_g, norm2_g, w_in, q_norm_g, k_norm_g, attn_sink, lru_conv_w, lru_conv_b, lru_w_a, lru_b_a, lru_w_x, lru_b_x, lru_lambda, attn_out_g, lru_out_g, w_out, ffn_w_gate, ffn_w_up, ffn_conv_w, ffn_conv_b, ffn_w_down):
    depth, d, _ = w_in.shape
    n_prompt = c_prompt.shape[0]
    n_req = n_prompt + c_sample.shape[0]
    rows = -(-n_req // SUBLANES) * SUBLANES
    c_all = jnp.concatenate([c_prompt, c_sample, jnp.zeros((rows - n_req, d), F32)], axis=0)
    mod = _modulation(c_all, w_mod, b_mod).reshape(depth, rows, 6, d)
    bias = _bias_table(rel_bias)

    w_in_b = w_in.astype(BF16)
    w_out_b = w_out.astype(BF16)
    w_gate_b = ffn_w_gate.astype(BF16)
    w_up_b = ffn_w_up.astype(BF16)
    w_down_b = ffn_w_down.astype(BF16)
    w_cat = jnp.concatenate([lru_w_a[:, 0], lru_w_a[:, 1], lru_w_x[:, 0], lru_w_x[:, 1]], axis=-1)
    w_cat = (0.5 * w_cat).astype(BF16)
    bw = LRU_BLOCK_DIM
    b_cat = jnp.concatenate(
        [lru_b_a.reshape(depth, 2, LRU_BLOCKS, bw), lru_b_x.reshape(depth, 2, LRU_BLOCKS, bw)], axis=1)
    b_cat = 0.5 * jnp.transpose(b_cat, (0, 2, 1, 3)).reshape(depth, LRU_BLOCKS, 1, 4 * bw)
    lam = jnp.transpose(lru_lambda.reshape(depth, 2, LRU_BLOCKS, bw), (0, 2, 1, 3))
    sink_rows = jnp.repeat(attn_sink.reshape(depth, N_KV_HEADS, GROUP), WINDOW, axis=-1)[..., None]
    sink_rows = jnp.broadcast_to(sink_rows, (depth, N_KV_HEADS, GROUP * WINDOW, HEAD_DIM))

    def run(x, row0):
        b, seq, _ = x.shape
        x2 = x.reshape(b * seq, d)
        for l in range(depth):
            q, k, v, xr, yg = _in_proj(x2, mod[l], row0, seq, norm1_g[l], w_in_b, l, q_norm_g[l], k_norm_g[l])
            sh = lambda a: a.reshape(a.shape[0], b, seq, a.shape[-1])
            flat = lambda a: a.reshape(a.shape[0], b * seq, a.shape[-1])
            attn = _attention(sh(q), sh(k), sh(v), bias, sink_rows[l])
            lru = _lru(sh(xr), sh(yg), lru_conv_w[l], lru_conv_b[l], w_cat[l], b_cat[l], lam[l])
            x2, h2 = _out_proj(x2, flat(attn), flat(lru), mod[l], row0, seq,
                               attn_out_g[l], lru_out_g[l], w_out_b, l, norm2_g[l])
            x2 = _ffn(x2, h2, mod[l], row0, seq, w_gate_b, w_up_b, ffn_conv_w[l], ffn_conv_b[l], w_down_b, l)
        return x2.reshape(b, seq, d)

    return (run(x_prompt, 0), run(x_sample, n_prompt))
```

```python
import functools
import math

import jax
import jax.numpy as jnp
from jax import lax
from jax.experimental import pallas as pl
from jax.experimental.pallas import tpu as pltpu

F32 = jnp.float32
BF16 = jnp.bfloat16

EPS = 1e-6
HEAD_DIM = 128
N_HEADS = 8
N_KV_HEADS = 2
GROUP = N_HEADS // N_KV_HEADS
WINDOW = 128
N_BUCKETS = 32
LRU_BLOCKS = 8
LRU_BLOCK_DIM = 128
LRU_CONV = 4
LRU_C = 8.0
FFN_CONV = 3
NEG_INF = -1e30
LOG2E = math.log2(math.e)
LN2 = math.log(2.0)
SERIES_LIMIT = 0.02
F32_TINY = float(jnp.finfo(jnp.float32).tiny)

SUBLANES = 8
LANES = 128
MIB = 1024 * 1024

TOKEN_TILE = 512
FFN_TOKEN_TILE = 512
FF_CHUNK = 1024
MOD_CHUNK = 1024
ATTN_Q_TILE = 2048
LRU_CHUNK = 512
LRU_SCAN_UNROLL = 12
LRU_SCAN_BATCH = 6
LRU_LOOKAHEAD = 3
ATTN_UNROLL = 4
HALO = SUBLANES


def _cparams(semantics, vmem_mib):
    return pltpu.CompilerParams(dimension_semantics=semantics, vmem_limit_bytes=vmem_mib * MIB)


def _dot(a, b):
    return jnp.dot(a, b, preferred_element_type=F32)


def _rms(x, gain):
    return x * lax.rsqrt(jnp.mean(x * x, axis=-1, keepdims=True) + EPS) * gain


def _gelu_tanh(x):
    return 0.5 * x * (1.0 + jnp.tanh(math.sqrt(2.0 / math.pi) * (x + 0.044715 * (x * x * x))))


def _sigmoid(x):
    return 1.0 / (1.0 + jnp.exp(-x))


def _one_minus_exp(x, exp_x):
    series = x * (-1.0 + x * (-1 / 2 + x * (-1 / 6 + x * (-1 / 24))))
    return jnp.where(x > -SERIES_LIMIT, series, 1.0 - exp_x)


def _mod_kernel(c_ref, w_ref, b_ref, o_ref):
    c = c_ref[...]
    cs = (c * _sigmoid(c)).astype(BF16)
    o_ref[0] = _dot(cs, w_ref[0].astype(BF16)) + b_ref[0]


def _modulation(c_all, w_mod, b_mod):
    depth, d, n = w_mod.shape
    rows = c_all.shape[0]
    return pl.pallas_call(
        _mod_kernel,
        grid=(depth, n // MOD_CHUNK),
        in_specs=[
            pl.BlockSpec((rows, d), lambda l, j: (0, 0)),
            pl.BlockSpec((1, d, MOD_CHUNK), lambda l, j: (l, 0, j)),
            pl.BlockSpec((1, 1, MOD_CHUNK), lambda l, j: (l, 0, j)),
        ],
        out_specs=pl.BlockSpec((1, rows, MOD_CHUNK), lambda l, j: (l, 0, j)),
        out_shape=jax.ShapeDtypeStruct((depth, rows, n), F32),
        compiler_params=_cparams(("arbitrary", "arbitrary"), 40),
        name="adaln_modulation",
    )(c_all, w_mod, b_mod.reshape(depth, 1, n))


_BUCKET_THRESHOLDS = (12, 16, 23, 32, 46, 64, 91)


def _bias_kernel(rb_ref, o_ref):
    head = pl.program_id(0)
    qi = lax.broadcasted_iota(jnp.int32, (WINDOW, 3 * WINDOW), 0)
    kj = lax.broadcasted_iota(jnp.int32, (WINDOW, 3 * WINDOW), 1)
    rel = kj - WINDOW - qi
    n = jnp.abs(rel)
    half = N_BUCKETS // 2
    large = jnp.full_like(n, half // 2)
    for t in _BUCKET_THRESHOLDS:
        large = large + jnp.where(n >= t, 1, 0)
    bucket = jnp.where(rel > 0, half, 0) + jnp.where(n < half // 2, n, large)
    bias = jnp.zeros((WINDOW, 3 * WINDOW), F32)
    for b in range(N_BUCKETS):
        bias = jnp.where(bucket == b, rb_ref[b, head], bias)
    o_ref[0] = jnp.where(n <= WINDOW, bias * LOG2E, NEG_INF)


def _bias_table(rel_bias):
    out = pl.pallas_call(
        _bias_kernel,
        grid=(N_HEADS,),
        in_specs=[pl.BlockSpec(memory_space=pltpu.SMEM)],
        out_specs=pl.BlockSpec((1, WINDOW, 3 * WINDOW), lambda h: (h, 0, 0)),
        out_shape=jax.ShapeDtypeStruct((N_HEADS, WINDOW, 3 * WINDOW), F32),
        compiler_params=_cparams(("arbitrary",), 16),
        name="rel_bias_table",
    )(rel_bias)
    return out.reshape(N_KV_HEADS, GROUP * WINDOW, 3 * WINDOW)


def _in_kernel(x_ref, mod_ref, g_ref, w_ref, qg_ref, kg_ref, q_ref, k_ref, v_ref, xr_ref, yg_ref):
    mod = mod_ref[0]
    h = _rms(x_ref[...], g_ref[...]) * (1.0 + mod[1:2]) + mod[0:1]
    hb = h.astype(BF16)
    nc = GROUP * HEAD_DIM
    aw = N_KV_HEADS * nc
    kvw = N_KV_HEADS * HEAD_DIM
    lw = LRU_BLOCKS * LRU_BLOCK_DIM
    per = nc // LRU_BLOCK_DIM

    def lru_part(dst_ref, base, act):
        for c in range(LRU_BLOCKS // per):
            z = _dot(hb, w_ref[:, base + c * nc:base + (c + 1) * nc])
            z = z if act is None else act(z)
            for i in range(per):
                dst_ref[c * per + i] = z[:, i * LRU_BLOCK_DIM:(i + 1) * LRU_BLOCK_DIM]

    lru_part(yg_ref, aw + 2 * kvw + lw, _gelu_tanh)
    for c in range(N_KV_HEADS):
        z = _dot(hb, w_ref[:, c * nc:(c + 1) * nc])
        for i in range(GROUP):
            zh = z[:, i * HEAD_DIM:(i + 1) * HEAD_DIM]
            q_ref[c, :, i * HEAD_DIM:(i + 1) * HEAD_DIM] = _rms(zh, qg_ref[...]).astype(BF16)
    z = _dot(hb, w_ref[:, aw:aw + 2 * kvw])
    for i in range(N_KV_HEADS):
        k_ref[i] = _rms(z[:, i * HEAD_DIM:(i + 1) * HEAD_DIM], kg_ref[...]).astype(BF16)
        v_ref[i] = z[:, kvw + i * HEAD_DIM:kvw + (i + 1) * HEAD_DIM].astype(BF16)
    lru_part(xr_ref, aw + 2 * kvw, None)


def _in_proj(x2, mod6, row0, seq, norm_g, w_in, layer, q_g, k_g):
    t, d = x2.shape
    gw = GROUP * HEAD_DIM
    tm = TOKEN_TILE
    tiles_per_seq = seq // tm
    blocked = lambda n, w: pl.BlockSpec((n, tm, w), lambda i: (0, i, 0))
    const = lambda shape: pl.BlockSpec(shape, lambda i: (0,) * len(shape))
    return pl.pallas_call(
        _in_kernel,
        grid=(t // tm,),
        in_specs=[
            pl.BlockSpec((tm, d), lambda i: (i, 0)),
            pl.BlockSpec((1, 6, d), lambda i: (row0 + i // tiles_per_seq, 0, 0)),
            const((1, d)),
            pl.BlockSpec((None,) + w_in.shape[1:], lambda i: (layer, 0, 0)),
            const((1, HEAD_DIM)),
            const((1, HEAD_DIM)),
        ],
        out_specs=[blocked(N_KV_HEADS, gw), blocked(N_KV_HEADS, HEAD_DIM), blocked(N_KV_HEADS, HEAD_DIM),
                   blocked(LRU_BLOCKS, LRU_BLOCK_DIM), blocked(LRU_BLOCKS, LRU_BLOCK_DIM)],
        out_shape=[
            jax.ShapeDtypeStruct((N_KV_HEADS, t, gw), BF16),
            jax.ShapeDtypeStruct((N_KV_HEADS, t, HEAD_DIM), BF16),
            jax.ShapeDtypeStruct((N_KV_HEADS, t, HEAD_DIM), BF16),
            jax.ShapeDtypeStruct((LRU_BLOCKS, t, LRU_BLOCK_DIM), F32),
            jax.ShapeDtypeStruct((LRU_BLOCKS, t, LRU_BLOCK_DIM), F32),
        ],
        compiler_params=_cparams(("parallel",), 52),
        name="in_proj",
    )(x2, mod6, norm_g.reshape(1, d), w_in, q_g.reshape(1, HEAD_DIM), k_g.reshape(1, HEAD_DIM))


def _attn_kernel(q_ref, k_ref, v_ref, bias_ref, sink_ref, o_ref, *, seq):
    nb = seq // WINDOW
    blocks = q_ref.shape[0] // WINDOW
    chunk = pl.program_id(2)
    scale = HEAD_DIM ** -0.5 * LOG2E
    sink = sink_ref[0] * LOG2E
    col = lax.broadcasted_iota(jnp.int32, (1, 3 * WINDOW), 1)
    ones = jnp.ones((3 * WINDOW, HEAD_DIM), BF16)
    rows = GROUP * WINDOW

    def scores(j):
        n = chunk * blocks + j
        r0 = j * WINDOW
        qs = jnp.concatenate(
            [q_ref[pl.ds(r0, WINDOW), g * HEAD_DIM:(g + 1) * HEAD_DIM] for g in range(GROUP)], axis=0)
        starts = [pl.multiple_of(jnp.maximum(n - 1, 0) * WINDOW, WINDOW),
                  pl.multiple_of(n * WINDOW, WINDOW),
                  pl.multiple_of(jnp.minimum(n + 1, nb - 1) * WINDOW, WINDOW)]
        kw = jnp.concatenate([k_ref[pl.ds(s, WINDOW), :] for s in starts], axis=0)
        vw = jnp.concatenate(
            [jnp.concatenate([v_ref[pl.ds(s, WINDOW), :] for s in starts], axis=0), ones], axis=1)
        s = lax.dot_general(qs, kw, (((1,), (1,)), ((), ())), preferred_element_type=F32)
        return n, r0, s, vw

    def softmax(n, r0, s, vw):
        s = s * scale + bias_ref[0]
        valid = ((col >= WINDOW) | (n > 0)) & ((col < 2 * WINDOW) | (n < nb - 1))
        s = jnp.where(valid, s, NEG_INF)
        m = jnp.maximum(jnp.broadcast_to(jnp.max(s, axis=-1, keepdims=True), (rows, HEAD_DIM)), sink)
        p = jnp.exp2(s - jnp.concatenate([m, m, m], axis=1)).astype(BF16)
        return r0, p, jnp.exp2(sink - m), vw

    def finish(r0, p, sink_term, vw):
        ov = _dot(p, vw)
        o = ov[:, :HEAD_DIM] / (ov[:, HEAD_DIM:] + sink_term)
        for g in range(GROUP):
            o_ref[pl.ds(r0, WINDOW), g * HEAD_DIM:(g + 1) * HEAD_DIM] = o[g * WINDOW:(g + 1) * WINDOW]

    groups = [range(g0, g0 + ATTN_UNROLL) for g0 in range(0, blocks, ATTN_UNROLL)]
    stage = [scores(j) for j in groups[0]]
    for g in range(len(groups)):
        ahead = [scores(j) for j in groups[g + 1]] if g + 1 < len(groups) else None
        for st in [softmax(*st) for st in stage]:
            finish(*st)
        stage = ahead


def _attention(q, k, v, bias, sink_rows):
    _, b, seq, gw = q.shape
    tq = min(ATTN_Q_TILE, seq)
    seq_blk = pl.BlockSpec((None, None, seq, HEAD_DIM), lambda bi, h, c: (h, bi, 0, 0))
    return pl.pallas_call(
        functools.partial(_attn_kernel, seq=seq),
        grid=(b, N_KV_HEADS, seq // tq),
        in_specs=[
            pl.BlockSpec((None, None, tq, gw), lambda bi, h, c: (h, bi, c, 0)),
            seq_blk,
            seq_blk,
            pl.BlockSpec((1, GROUP * WINDOW, 3 * WINDOW), lambda bi, h, c: (h, 0, 0)),
            pl.BlockSpec((1, GROUP * WINDOW, HEAD_DIM), lambda bi, h, c: (h, 0, 0)),
        ],
        out_specs=pl.BlockSpec((None, None, tq, gw), lambda bi, h, c: (h, bi, c, 0)),
        out_shape=jax.ShapeDtypeStruct((N_KV_HEADS, b, seq, gw), F32),
        compiler_params=_cparams(("parallel", "parallel", "arbitrary"), 40),
        name="window_attention",
    )(q, k, v, bias, sink_rows)


def _lru_kernel(xr_ref, yg_ref, cw_ref, cb_ref, w_ref, gb_ref, lam_ref, o_ref,
                xp_ref, af_ref, bf_ref, ab_ref, bb_ref, carry_ref, *, seq, pitch):
    tc = min(LRU_CHUNK, seq)
    n_chunks = seq // tc
    pad = SUBLANES
    zeros8 = jnp.zeros((pad, LANES), F32)
    xp_ref[0:pad, :] = zeros8
    xp_ref[seq + pad:seq + 2 * pad, :] = zeros8
    tail = SUBLANES * pitch - seq
    for r in (af_ref, bf_ref, ab_ref, bb_ref):
        r[seq:seq + tail, :] = jnp.zeros((tail, LANES), F32)

    def copy_body(c, carry):
        t0 = pl.multiple_of(c * tc, tc)
        xp_ref[pl.ds(t0 + pad, tc), :] = xr_ref[pl.ds(t0, tc), :]
        return carry

    lax.fori_loop(0, n_chunks, copy_body, 0)

    half_log2_decay = (-0.5 * LRU_C * LOG2E) * jax.nn.softplus(-lam_ref[0])
    cw = cw_ref[...]
    cb = cb_ref[...]
    gb = gb_ref[0]
    bw = LRU_BLOCK_DIM

    def gate_body(c, carry):
        t0 = pl.multiple_of(c * tc, tc)
        xc = xp_ref[pl.ds(t0 + pad - 2, tc), :] * cw[0:1] + cb
        for kk in range(1, LRU_CONV):
            xc = xc + xp_ref[pl.ds(t0 + pad - 2 + kk, tc), :] * cw[kk:kk + 1]
        t = jnp.tanh(_dot(xc.astype(BF16), w_ref[0]) + gb)
        xh = 0.5 * xc
        for d, (a_ref, b_ref) in enumerate(((af_ref, bf_ref), (ab_ref, bb_ref))):
            log2_a = (1.0 + t[:, d * bw:(d + 1) * bw]) * half_log2_decay[d:d + 1]
            a = jnp.exp2(log2_a)
            a_ref[pl.ds(t0, tc), :] = a
            y = _one_minus_exp(log2_a * (2.0 * LN2), a * a)
            root = y * lax.rsqrt(jnp.maximum(y, F32_TINY))
            b_ref[pl.ds(t0, tc), :] = root * (1.0 + t[:, (2 + d) * bw:(3 + d) * bw]) * xh
        return carry

    lax.fori_loop(0, n_chunks, gate_body, 0)

    def seg(i):
        return pl.ds(i, SUBLANES, stride=pitch)

    def scan_steps(a_ref, b_ref, steps, h, p):
        ab = [(a_ref[seg(i), :], b_ref[seg(i), :]) for i in steps]
        for g0 in range(0, len(steps), LRU_LOOKAHEAD):
            a_run = b_run = None
            for i, (a, b) in zip(steps[g0:g0 + LRU_LOOKAHEAD], ab[g0:g0 + LRU_LOOKAHEAD]):
                a_run, b_run = (a, b) if a_run is None else (a * a_run, a * b_run + b)
                h_i = a_run * h + b_run
                p_i = a_run * p
                b_ref[seg(i), :] = h_i
                a_ref[seg(i), :] = p_i
            h, p = h_i, p_i
        return h, p

    def scan_body(it, carry):
        hf, pf, hb, pb = carry
        for k0 in range(0, LRU_SCAN_UNROLL, LRU_SCAN_BATCH):
            i0 = it * LRU_SCAN_UNROLL + k0
            hf, pf = scan_steps(af_ref, bf_ref, [i0 + k for k in range(LRU_SCAN_BATCH)], hf, pf)
            hb, pb = scan_steps(ab_ref, bb_ref, [pitch - 1 - i0 - k for k in range(LRU_SCAN_BATCH)], hb, pb)
        return hf, pf, hb, pb

    zero = jnp.zeros((SUBLANES, LANES), F32)
    one = jnp.ones((SUBLANES, LANES), F32)
    hf, pf, hb, pb = lax.fori_loop(0, pitch // LRU_SCAN_UNROLL, scan_body, (zero, one, zero, one))

    row = lax.broadcasted_iota(jnp.int32, (SUBLANES, LANES), 0)
    cf = zero
    c = jnp.zeros((1, LANES), F32)
    for s in range(1, SUBLANES):
        c = pf[s - 1:s] * c + hf[s - 1:s]
        cf = jnp.where(row == s, c, cf)
    cbk = zero
    c = jnp.zeros((1, LANES), F32)
    for s in range(SUBLANES - 2, -1, -1):
        c = pb[s + 1:s + 2] * c + hb[s + 1:s + 2]
        cbk = jnp.where(row == s, c, cbk)
    carry_ref[0:SUBLANES, :] = cf
    carry_ref[SUBLANES:2 * SUBLANES, :] = cbk

    to = min(LRU_CHUNK, seq // SUBLANES)
    rows = lax.broadcasted_iota(jnp.int32, (to, LANES), 0)

    def out_body(c, carry):
        t0 = pl.multiple_of(c * to, to)
        s0 = lax.div(t0, pitch)
        s1 = jnp.minimum(s0 + 1, SUBLANES - 1)
        in_first = rows < (s0 + 1) * pitch - t0
        cf_rows = jnp.where(in_first, carry_ref[pl.ds(s0, 1), :], carry_ref[pl.ds(s1, 1), :])
        cb_rows = jnp.where(in_first, carry_ref[pl.ds(SUBLANES + s0, 1), :],
                            carry_ref[pl.ds(SUBLANES + s1, 1), :])
        sl = pl.ds(t0, to)
        h = bf_ref[sl, :] + af_ref[sl, :] * cf_rows + bb_ref[sl, :] + ab_ref[sl, :] * cb_rows
        o_ref[sl, :] = h * yg_ref[sl, :]
        return carry

    lax.fori_loop(0, seq // to, out_body, 0)


def _lru(xr, yg, conv_w, conv_b, w_cat, b_cat, lam):
    _, b, seq, bw = xr.shape
    lw = LRU_BLOCKS * bw
    pitch = seq // SUBLANES
    while pitch % (2 * LRU_SCAN_UNROLL) != LRU_SCAN_UNROLL:
        pitch += 1
    seq_blk = pl.BlockSpec((None, None, seq, bw), lambda bi, n: (n, bi, 0, 0))
    scan_rows = SUBLANES * pitch
    return pl.pallas_call(
        functools.partial(_lru_kernel, seq=seq, pitch=pitch),
        grid=(b, LRU_BLOCKS),
        in_specs=[
            seq_blk,
            seq_blk,
            pl.BlockSpec((LRU_CONV, bw), lambda bi, n: (0, n)),
            pl.BlockSpec((1, bw), lambda bi, n: (0, n)),
            pl.BlockSpec((1, bw, 4 * bw), lambda bi, n: (n, 0, 0)),
            pl.BlockSpec((1, 1, 4 * bw), lambda bi, n: (n, 0, 0)),
            pl.BlockSpec((1, 2, bw), lambda bi, n: (n, 0, 0)),
        ],
        out_specs=seq_blk,
        out_shape=jax.ShapeDtypeStruct((LRU_BLOCKS, b, seq, bw), F32),
        scratch_shapes=[
            pltpu.VMEM((seq + 2 * SUBLANES, LANES), F32),
            pltpu.VMEM((scan_rows, LANES), F32),
            pltpu.VMEM((scan_rows, LANES), F32),
            pltpu.VMEM((scan_rows, LANES), F32),
            pltpu.VMEM((scan_rows, LANES), F32),
            pltpu.VMEM((2 * SUBLANES, LANES), F32),
        ],
        compiler_params=_cparams(("parallel", "parallel"), 52),
        name="rglru",
    )(xr, yg, conv_w, conv_b.reshape(1, lw), w_cat, b_cat, lam)


def _out_kernel(x_ref, attn_ref, lru_ref, mod_ref, ag_ref, lg_ref, w_ref, n2_ref, o_ref, h_ref):
    attn = jnp.concatenate([attn_ref[c] for c in range(attn_ref.shape[0])], axis=1)
    lru = jnp.concatenate([lru_ref[c] for c in range(lru_ref.shape[0])], axis=1)
    na = _rms(attn, ag_ref[...]).astype(BF16)
    nl = _rms(lru, lg_ref[...]).astype(BF16)
    y = _dot(jnp.concatenate([na, nl], axis=1), w_ref[...])
    mod = mod_ref[0]
    x1 = x_ref[...] + mod[2:3] * y
    o_ref[...] = x1
    h_ref[...] = (_rms(x1, n2_ref[...]) * (1.0 + mod[4:5]) + mod[3:4]).astype(BF16)


def _out_proj(x2, attn3, lru3, mod6, row0, seq, attn_g, lru_g, w_out, layer, norm2_g):
    t, d = x2.shape
    aw = attn3.shape[0] * attn3.shape[2]
    lw = lru3.shape[0] * lru3.shape[2]
    tm = TOKEN_TILE
    tiles_per_seq = seq // tm
    tok = lambda w: pl.BlockSpec((tm, w), lambda i: (i, 0))
    blocked = lambda a: pl.BlockSpec((a.shape[0], tm, a.shape[2]), lambda i: (0, i, 0))
    const = lambda shape: pl.BlockSpec(shape, lambda i: (0,) * len(shape))
    return pl.pallas_call(
        _out_kernel,
        grid=(t // tm,),
        in_specs=[
            tok(d), blocked(attn3), blocked(lru3),
            pl.BlockSpec((1, 6, d), lambda i: (row0 + i // tiles_per_seq, 0, 0)),
            const((1, aw)), const((1, lw)),
            pl.BlockSpec((None,) + w_out.shape[1:], lambda i: (layer, 0, 0)),
            const((1, d)),
        ],
        out_specs=[tok(d), tok(d)],
        out_shape=[jax.ShapeDtypeStruct((t, d), F32), jax.ShapeDtypeStruct((t, d), BF16)],
        compiler_params=_cparams(("parallel",), 52),
        name="out_proj",
    )(x2, attn3, lru3, mod6, attn_g.reshape(1, aw), lru_g.reshape(1, lw), w_out, norm2_g.reshape(1, d))


def _halo_kernel(h_ref, w_ref, o_ref):
    o_ref[...] = _dot(h_ref[...], w_ref[...])


def _ffn_halo(h2, seq, tm, w_gate, layer):
    t, d = h2.shape
    f = w_gate.shape[2]
    n_tiles = t // tm
    tiles_per_seq = seq // tm
    tiles = h2.reshape(n_tiles, tm, d)
    zero_row = jnp.zeros((1, d), h2.dtype)
    pos = jnp.arange(n_tiles) % tiles_per_seq
    prev = jnp.concatenate([zero_row, tiles[:-1, tm - 1]], axis=0)
    prev = jnp.where((pos == 0)[:, None], 0, prev)
    nxt = jnp.concatenate([tiles[1:, 0], zero_row], axis=0)
    nxt = jnp.where((pos == tiles_per_seq - 1)[:, None], 0, nxt)
    rows = jnp.stack([prev, nxt], axis=1).reshape(2 * n_tiles, d)
    out = pl.pallas_call(
        _halo_kernel,
        grid=(f // FF_CHUNK,),
        in_specs=[
            pl.BlockSpec((2 * n_tiles, d), lambda j: (0, 0)),
            pl.BlockSpec((None, d, FF_CHUNK), lambda j: (layer, 0, j)),
        ],
        out_specs=pl.BlockSpec((2 * n_tiles, FF_CHUNK), lambda j: (0, j)),
        out_shape=jax.ShapeDtypeStruct((2 * n_tiles, f), F32),
        compiler_params=_cparams(("arbitrary",), 32),
        name="ffn_halo",
    )(rows, w_gate)
    return out.reshape(n_tiles, 2, f)


def _ffn_kernel(x_ref, h_ref, gh_ref, mod_ref, wg_ref, wu_ref, cw_ref, cb_ref, wd_ref, o_ref, ge_ref):
    j = pl.program_id(1)
    tm = x_ref.shape[0]
    fc = wg_ref.shape[1]

    @pl.when(j == 0)
    def _():
        o_ref[...] = jnp.zeros_like(o_ref)

    cols = pl.ds(pl.multiple_of(j * fc, fc), fc)
    h = h_ref[...]
    ge_ref[HALO:HALO + tm, :] = _dot(h, wg_ref[...])
    ge_ref[HALO - 1:HALO, :] = gh_ref[0, 0:1, cols]
    ge_ref[HALO + tm:HALO + tm + 1, :] = gh_ref[0, 1:2, cols]
    u = _dot(h, wu_ref[...])
    cw = cw_ref[:, cols]
    g = ge_ref[HALO - 1:HALO - 1 + tm, :] * cw[0:1] + cb_ref[:, cols]
    g = g + ge_ref[HALO:HALO + tm, :] * cw[1:2]
    g = g + ge_ref[HALO + 1:HALO + 1 + tm, :] * cw[2:3]
    act = (_gelu_tanh(g) * u).astype(BF16)
    o_ref[...] += _dot(act, wd_ref[...])

    @pl.when(j == pl.num_programs(1) - 1)
    def _():
        o_ref[...] = x_ref[...] + mod_ref[0][5:6] * o_ref[...]


def _ffn(x2, h2, mod6, row0, seq, w_gate, w_up, conv_w, conv_b, w_down, layer):
    t, d = x2.shape
    f = w_gate.shape[2]
    fc = FF_CHUNK
    tm = min(FFN_TOKEN_TILE, seq)
    tiles_per_seq = seq // tm
    ge_halo = _ffn_halo(h2, seq, tm, w_gate, layer)
    return pl.pallas_call(
        _ffn_kernel,
        grid=(t // tm, f // fc),
        in_specs=[
            pl.BlockSpec((tm, d), lambda i, j: (i, 0)),
            pl.BlockSpec((tm, d), lambda i, j: (i, 0)),
            pl.BlockSpec((1, 2, f), lambda i, j: (i, 0, 0)),
            pl.BlockSpec((1, 6, d), lambda i, j: (row0 + i // tiles_per_seq, 0, 0)),
            pl.BlockSpec((None, d, fc), lambda i, j: (layer, 0, j)),
            pl.BlockSpec((None, d, fc), lambda i, j: (layer, 0, j)),
            pl.BlockSpec((FFN_CONV, f), lambda i, j: (0, 0)),
            pl.BlockSpec((1, f), lambda i, j: (0, 0)),
            pl.BlockSpec((None, fc, d), lambda i, j: (layer, j, 0)),
        ],
        out_specs=pl.BlockSpec((tm, d), lambda i, j: (i, 0)),
        out_shape=jax.ShapeDtypeStruct((t, d), F32),
        scratch_shapes=[pltpu.VMEM((tm + 2 * HALO, fc), F32)],
        compiler_params=_cparams(("parallel", "arbitrary"), 57),
        name="convglu_ffn",
    )(x2, h2, ge_halo, mod6, w_gate, w_up, conv_w, conv_b.reshape(1, f), w_down)


def kernel(x_prompt, x_sample, c_prompt, c_sample, rel_bias, w_mod, b_mod, norm1_g, norm2_g, w_in, q_norm_g, k_norm_g, attn_sink, lru_conv_w, lru_conv_b, lru_w_a, lru_b_a, lru_w_x, lru_b_x, lru_lambda, attn_out_g, lru_out_g, w_out, ffn_w_gate, ffn_w_up, ffn_conv_w, ffn_conv_b, ffn_w_down):
    depth, d, _ = w_in.shape
    n_prompt = c_prompt.shape[0]
    n_req = n_prompt + c_sample.shape[0]
    rows = -(-n_req // SUBLANES) * SUBLANES
    c_all = jnp.concatenate([c_prompt, c_sample, jnp.zeros((rows - n_req, d), F32)], axis=0)
    mod = _modulation(c_all, w_mod, b_mod).reshape(depth, rows, 6, d)
    bias = _bias_table(rel_bias)

    w_in_b = w_in.astype(BF16)
    w_out_b = w_out.astype(BF16)
    w_gate_b = ffn_w_gate.astype(BF16)
    w_up_b = ffn_w_up.astype(BF16)
    w_down_b = ffn_w_down.astype(BF16)
    w_cat = jnp.concatenate([lru_w_a[:, 0], lru_w_a[:, 1], lru_w_x[:, 0], lru_w_x[:, 1]], axis=-1)
    w_cat = (0.5 * w_cat).astype(BF16)
    bw = LRU_BLOCK_DIM
    b_cat = jnp.concatenate(
        [lru_b_a.reshape(depth, 2, LRU_BLOCKS, bw), lru_b_x.reshape(depth, 2, LRU_BLOCKS, bw)], axis=1)
    b_cat = 0.5 * jnp.transpose(b_cat, (0, 2, 1, 3)).reshape(depth, LRU_BLOCKS, 1, 4 * bw)
    lam = jnp.transpose(lru_lambda.reshape(depth, 2, LRU_BLOCKS, bw), (0, 2, 1, 3))
    sink_rows = jnp.repeat(attn_sink.reshape(depth, N_KV_HEADS, GROUP), WINDOW, axis=-1)[..., None]
    sink_rows = jnp.broadcast_to(sink_rows, (depth, N_KV_HEADS, GROUP * WINDOW, HEAD_DIM))

    def run(x, row0):
        b, seq, _ = x.shape
        x2 = x.reshape(b * seq, d)
        for l in range(depth):
            q, k, v, xr, yg = _in_proj(x2, mod[l], row0, seq, norm1_g[l], w_in_b, l, q_norm_g[l], k_norm_g[l])
            sh = lambda a: a.reshape(a.shape[0], b, seq, a.shape[-1])
            flat = lambda a: a.reshape(a.shape[0], b * seq, a.shape[-1])
            attn = _attention(sh(q), sh(k), sh(v), bias, sink_rows[l])
            lru = _lru(sh(xr), sh(yg), lru_conv_w[l], lru_conv_b[l], w_cat[l], b_cat[l], lam[l])
            x2, h2 = _out_proj(x2, flat(attn), flat(lru), mod[l], row0, seq,
                               attn_out_g[l], lru_out_g[l], w_out_b, l, norm2_g[l])
            x2 = _ffn(x2, h2, mod[l], row0, seq, w_gate_b, w_up_b, ffn_conv_w[l], ffn_conv_b[l], w_down_b, l)
        return x2.reshape(b, seq, d)

    return (run(x_prompt, 0), run(x_sample, n_prompt))
```

```python
import functools
import math

import jax
import jax.numpy as jnp
from jax import lax
from jax.experimental import pallas as pl
from jax.experimental.pallas import tpu as pltpu

F32 = jnp.float32
BF16 = jnp.bfloat16

EPS = 1e-6
HEAD_DIM = 128
N_HEADS = 8
N_KV_HEADS = 2
GROUP = N_HEADS // N_KV_HEADS
WINDOW = 128
N_BUCKETS = 32
LRU_BLOCKS = 8
LRU_BLOCK_DIM = 128
LRU_CONV = 4
LRU_C = 8.0
FFN_CONV = 3
NEG_INF = -1e30
LOG2E = math.log2(math.e)
LN2 = math.log(2.0)
SERIES_LIMIT = 0.01
F32_TINY = float(jnp.finfo(jnp.float32).tiny)

SUBLANES = 8
LANES = 128
MIB = 1024 * 1024

TOKEN_TILE = 512
FFN_TOKEN_TILE = 512
FF_CHUNK = 1024
MOD_CHUNK = 1024
ATTN_Q_TILE = 2048
LRU_CHUNK = 512
LRU_SCAN_UNROLL = 12
LRU_SCAN_BATCH = 6
LRU_LOOKAHEAD = 3
ATTN_UNROLL = 4
HALO = SUBLANES


def _cparams(semantics, vmem_mib):
    return pltpu.CompilerParams(dimension_semantics=semantics, vmem_limit_bytes=vmem_mib * MIB)


def _dot(a, b):
    return jnp.dot(a, b, preferred_element_type=F32)


def _rms(x, gain):
    return x * lax.rsqrt(jnp.mean(x * x, axis=-1, keepdims=True) + EPS) * gain


def _gelu_tanh(x):
    return 0.5 * x * (1.0 + jnp.tanh(math.sqrt(2.0 / math.pi) * (x + 0.044715 * (x * x * x))))


def _sigmoid(x):
    return 1.0 / (1.0 + jnp.exp(-x))


def _one_minus_exp(x, exp_x):
    series = x * (-1.0 + x * (-1 / 2 + x * (-1 / 6)))
    return jnp.where(x > -SERIES_LIMIT, series, 1.0 - exp_x)


def _mod_kernel(c_ref, w_ref, b_ref, o_ref):
    c = c_ref[...]
    cs = (c * _sigmoid(c)).astype(BF16)
    o_ref[0] = _dot(cs, w_ref[0].astype(BF16)) + b_ref[0]


def _modulation(c_all, w_mod, b_mod):
    depth, d, n = w_mod.shape
    rows = c_all.shape[0]
    return pl.pallas_call(
        _mod_kernel,
        grid=(depth, n // MOD_CHUNK),
        in_specs=[
            pl.BlockSpec((rows, d), lambda l, j: (0, 0)),
            pl.BlockSpec((1, d, MOD_CHUNK), lambda l, j: (l, 0, j)),
            pl.BlockSpec((1, 1, MOD_CHUNK), lambda l, j: (l, 0, j)),
        ],
        out_specs=pl.BlockSpec((1, rows, MOD_CHUNK), lambda l, j: (l, 0, j)),
        out_shape=jax.ShapeDtypeStruct((depth, rows, n), F32),
        compiler_params=_cparams(("arbitrary", "arbitrary"), 40),
        name="adaln_modulation",
    )(c_all, w_mod, b_mod.reshape(depth, 1, n))


_BUCKET_THRESHOLDS = (12, 16, 23, 32, 46, 64, 91)


def _bias_kernel(rb_ref, o_ref):
    head = pl.program_id(0)
    qi = lax.broadcasted_iota(jnp.int32, (WINDOW, 3 * WINDOW), 0)
    kj = lax.broadcasted_iota(jnp.int32, (WINDOW, 3 * WINDOW), 1)
    rel = kj - WINDOW - qi
    n = jnp.abs(rel)
    half = N_BUCKETS // 2
    large = jnp.full_like(n, half // 2)
    for t in _BUCKET_THRESHOLDS:
        large = large + jnp.where(n >= t, 1, 0)
    bucket = jnp.where(rel > 0, half, 0) + jnp.where(n < half // 2, n, large)
    bias = jnp.zeros((WINDOW, 3 * WINDOW), F32)
    for b in range(N_BUCKETS):
        bias = jnp.where(bucket == b, rb_ref[b, head], bias)
    o_ref[0] = jnp.where(n <= WINDOW, bias * LOG2E, NEG_INF)


def _bias_table(rel_bias):
    out = pl.pallas_call(
        _bias_kernel,
        grid=(N_HEADS,),
        in_specs=[pl.BlockSpec(memory_space=pltpu.SMEM)],
        out_specs=pl.BlockSpec((1, WINDOW, 3 * WINDOW), lambda h: (h, 0, 0)),
        out_shape=jax.ShapeDtypeStruct((N_HEADS, WINDOW, 3 * WINDOW), F32),
        compiler_params=_cparams(("arbitrary",), 16),
        name="rel_bias_table",
    )(rel_bias)
    return out.reshape(N_KV_HEADS, GROUP * WINDOW, 3 * WINDOW)


def _in_kernel(x_ref, mod_ref, g_ref, w_ref, qg_ref, kg_ref, q_ref, k_ref, v_ref, xr_ref, yg_ref):
    mod = mod_ref[0]
    h = _rms(x_ref[...], g_ref[...]) * (1.0 + mod[1:2]) + mod[0:1]
    hb = h.astype(BF16)
    nc = GROUP * HEAD_DIM
    aw = N_KV_HEADS * nc
    kvw = N_KV_HEADS * HEAD_DIM
    lw = LRU_BLOCKS * LRU_BLOCK_DIM
    per = nc // LRU_BLOCK_DIM

    def lru_part(dst_ref, base, act):
        for c in range(LRU_BLOCKS // per):
            z = _dot(hb, w_ref[:, base + c * nc:base + (c + 1) * nc])
            z = z if act is None else act(z)
            for i in range(per):
                dst_ref[c * per + i] = z[:, i * LRU_BLOCK_DIM:(i + 1) * LRU_BLOCK_DIM]

    lru_part(yg_ref, aw + 2 * kvw + lw, _gelu_tanh)
    for c in range(N_KV_HEADS):
        z = _dot(hb, w_ref[:, c * nc:(c + 1) * nc])
        for i in range(GROUP):
            zh = z[:, i * HEAD_DIM:(i + 1) * HEAD_DIM]
            q_ref[c, :, i * HEAD_DIM:(i + 1) * HEAD_DIM] = _rms(zh, qg_ref[...]).astype(BF16)
    z = _dot(hb, w_ref[:, aw:aw + 2 * kvw])
    for i in range(N_KV_HEADS):
        k_ref[i] = _rms(z[:, i * HEAD_DIM:(i + 1) * HEAD_DIM], kg_ref[...]).astype(BF16)
        v_ref[i] = z[:, kvw + i * HEAD_DIM:kvw + (i + 1) * HEAD_DIM].astype(BF16)
    lru_part(xr_ref, aw + 2 * kvw, None)


def _in_proj(x2, mod6, row0, seq, norm_g, w_in, layer, q_g, k_g):
    t, d = x2.shape
    gw = GROUP * HEAD_DIM
    tm = TOKEN_TILE
    tiles_per_seq = seq // tm
    blocked = lambda n, w: pl.BlockSpec((n, tm, w), lambda i: (0, i, 0))
    const = lambda shape: pl.BlockSpec(shape, lambda i: (0,) * len(shape))
    return pl.pallas_call(
        _in_kernel,
        grid=(t // tm,),
        in_specs=[
            pl.BlockSpec((tm, d), lambda i: (i, 0)),
            pl.BlockSpec((1, 6, d), lambda i: (row0 + i // tiles_per_seq, 0, 0)),
            const((1, d)),
            pl.BlockSpec((None,) + w_in.shape[1:], lambda i: (layer, 0, 0)),
            const((1, HEAD_DIM)),
            const((1, HEAD_DIM)),
        ],
        out_specs=[blocked(N_KV_HEADS, gw), blocked(N_KV_HEADS, HEAD_DIM), blocked(N_KV_HEADS, HEAD_DIM),
                   blocked(LRU_BLOCKS, LRU_BLOCK_DIM), blocked(LRU_BLOCKS, LRU_BLOCK_DIM)],
        out_shape=[
            jax.ShapeDtypeStruct((N_KV_HEADS, t, gw), BF16),
            jax.ShapeDtypeStruct((N_KV_HEADS, t, HEAD_DIM), BF16),
            jax.ShapeDtypeStruct((N_KV_HEADS, t, HEAD_DIM), BF16),
            jax.ShapeDtypeStruct((LRU_BLOCKS, t, LRU_BLOCK_DIM), F32),
            jax.ShapeDtypeStruct((LRU_BLOCKS, t, LRU_BLOCK_DIM), F32),
        ],
        compiler_params=_cparams(("parallel",), 52),
        name="in_proj",
    )(x2, mod6, norm_g.reshape(1, d), w_in, q_g.reshape(1, HEAD_DIM), k_g.reshape(1, HEAD_DIM))


def _attn_kernel(q_ref, k_ref, v_ref, bias_ref, sink_ref, o_ref, *, seq):
    nb = seq // WINDOW
    blocks = q_ref.shape[0] // WINDOW
    chunk = pl.program_id(2)
    scale = HEAD_DIM ** -0.5 * LOG2E
    sink = sink_ref[0] * LOG2E
    col = lax.broadcasted_iota(jnp.int32, (1, 3 * WINDOW), 1)
    ones = jnp.ones((3 * WINDOW, HEAD_DIM), BF16)
    rows = GROUP * WINDOW

    def scores(j):
        n = chunk * blocks + j
        r0 = j * WINDOW
        qs = jnp.concatenate(
            [q_ref[pl.ds(r0, WINDOW), g * HEAD_DIM:(g + 1) * HEAD_DIM] for g in range(GROUP)], axis=0)
        starts = [pl.multiple_of(jnp.maximum(n - 1, 0) * WINDOW, WINDOW),
                  pl.multiple_of(n * WINDOW, WINDOW),
                  pl.multiple_of(jnp.minimum(n + 1, nb - 1) * WINDOW, WINDOW)]
        kw = jnp.concatenate([k_ref[pl.ds(s, WINDOW), :] for s in starts], axis=0)
        vw = jnp.concatenate(
            [jnp.concatenate([v_ref[pl.ds(s, WINDOW), :] for s in starts], axis=0), ones], axis=1)
        s = lax.dot_general(qs, kw, (((1,), (1,)), ((), ())), preferred_element_type=F32)
        return n, r0, s, vw

    def softmax(n, r0, s, vw):
        s = s * scale + bias_ref[0]
        valid = ((col >= WINDOW) | (n > 0)) & ((col < 2 * WINDOW) | (n < nb - 1))
        s = jnp.where(valid, s, NEG_INF)
        m = jnp.maximum(jnp.broadcast_to(jnp.max(s, axis=-1, keepdims=True), (rows, HEAD_DIM)), sink)
        p = jnp.exp2(s - jnp.concatenate([m, m, m], axis=1)).astype(BF16)
        return r0, p, jnp.exp2(sink - m), vw

    def finish(r0, p, sink_term, vw):
        ov = _dot(p, vw)
        o = ov[:, :HEAD_DIM] / (ov[:, HEAD_DIM:] + sink_term)
        for g in range(GROUP):
            o_ref[pl.ds(r0, WINDOW), g * HEAD_DIM:(g + 1) * HEAD_DIM] = o[g * WINDOW:(g + 1) * WINDOW]

    groups = [range(g0, g0 + ATTN_UNROLL) for g0 in range(0, blocks, ATTN_UNROLL)]
    stage = [scores(j) for j in groups[0]]
    for g in range(len(groups)):
        ahead = [scores(j) for j in groups[g + 1]] if g + 1 < len(groups) else None
        for st in [softmax(*st) for st in stage]:
            finish(*st)
        stage = ahead


def _attention(q, k, v, bias, sink_rows):
    _, b, seq, gw = q.shape
    tq = min(ATTN_Q_TILE, seq)
    seq_blk = pl.BlockSpec((None, None, seq, HEAD_DIM), lambda bi, h, c: (h, bi, 0, 0))
    return pl.pallas_call(
        functools.partial(_attn_kernel, seq=seq),
        grid=(b, N_KV_HEADS, seq // tq),
        in_specs=[
            pl.BlockSpec((None, None, tq, gw), lambda bi, h, c: (h, bi, c, 0)),
            seq_blk,
            seq_blk,
            pl.BlockSpec((1, GROUP * WINDOW, 3 * WINDOW), lambda bi, h, c: (h, 0, 0)),
            pl.BlockSpec((1, GROUP * WINDOW, HEAD_DIM), lambda bi, h, c: (h, 0, 0)),
        ],
        out_specs=pl.BlockSpec((None, None, tq, gw), lambda bi, h, c: (h, bi, c, 0)),
        out_shape=jax.ShapeDtypeStruct((N_KV_HEADS, b, seq, gw), F32),
        compiler_params=_cparams(("parallel", "parallel", "arbitrary"), 40),
        name="window_attention",
    )(q, k, v, bias, sink_rows)


def _lru_kernel(xr_ref, yg_ref, cw_ref, cb_ref, w_ref, gb_ref, lam_ref, o_ref,
                xp_ref, af_ref, bf_ref, ab_ref, bb_ref, carry_ref, *, seq, pitch):
    tc = min(LRU_CHUNK, seq)
    n_chunks = seq // tc
    pad = SUBLANES
    zeros8 = jnp.zeros((pad, LANES), F32)
    xp_ref[0:pad, :] = zeros8
    xp_ref[seq + pad:seq + 2 * pad, :] = zeros8
    tail = SUBLANES * pitch - seq
    for r in (af_ref, bf_ref, ab_ref, bb_ref):
        r[seq:seq + tail, :] = jnp.zeros((tail, LANES), F32)

    def copy_body(c, carry):
        t0 = pl.multiple_of(c * tc, tc)
        xp_ref[pl.ds(t0 + pad, tc), :] = xr_ref[pl.ds(t0, tc), :]
        return carry

    lax.fori_loop(0, n_chunks, copy_body, 0)

    half_log2_decay = (-0.5 * LRU_C * LOG2E) * jax.nn.softplus(-lam_ref[0])
    cw = 0.5 * cw_ref[...]
    cb = 0.5 * cb_ref[...]
    gb = 0.5 * gb_ref[0]
    bw = LRU_BLOCK_DIM

    def gate_body(c, carry):
        t0 = pl.multiple_of(c * tc, tc)
        xh = xp_ref[pl.ds(t0 + pad - 2, tc), :] * cw[0:1] + cb
        for kk in range(1, LRU_CONV):
            xh = xh + xp_ref[pl.ds(t0 + pad - 2 + kk, tc), :] * cw[kk:kk + 1]
        t = jnp.tanh(_dot(xh.astype(BF16), w_ref[0]) + gb)
        for d, (a_ref, b_ref) in enumerate(((af_ref, bf_ref), (ab_ref, bb_ref))):
            log2_a = (1.0 + t[:, d * bw:(d + 1) * bw]) * half_log2_decay[d:d + 1]
            a = jnp.exp2(log2_a)
            a_ref[pl.ds(t0, tc), :] = a
            y = _one_minus_exp(log2_a * (2.0 * LN2), a * a)
            root = y * lax.rsqrt(jnp.maximum(y, F32_TINY))
            b_ref[pl.ds(t0, tc), :] = root * (1.0 + t[:, (2 + d) * bw:(3 + d) * bw]) * xh
        return carry

    lax.fori_loop(0, n_chunks, gate_body, 0)

    def seg(i):
        return pl.ds(i, SUBLANES, stride=pitch)

    def scan_steps(a_ref, b_ref, steps, h, p):
        ab = [(a_ref[seg(i), :], b_ref[seg(i), :]) for i in steps]
        for g0 in range(0, len(steps), LRU_LOOKAHEAD):
            a_run = b_run = None
            for i, (a, b) in zip(steps[g0:g0 + LRU_LOOKAHEAD], ab[g0:g0 + LRU_LOOKAHEAD]):
                a_run, b_run = (a, b) if a_run is None else (a * a_run, a * b_run + b)
                h_i = a_run * h + b_run
                p_i = a_run * p
                b_ref[seg(i), :] = h_i
                a_ref[seg(i), :] = p_i
            h, p = h_i, p_i
        return h, p

    def scan_body(it, carry):
        hf, pf, hb, pb = carry
        for k0 in range(0, LRU_SCAN_UNROLL, LRU_SCAN_BATCH):
            i0 = it * LRU_SCAN_UNROLL + k0
            hf, pf = scan_steps(af_ref, bf_ref, [i0 + k for k in range(LRU_SCAN_BATCH)], hf, pf)
            hb, pb = scan_steps(ab_ref, bb_ref, [pitch - 1 - i0 - k for k in range(LRU_SCAN_BATCH)], hb, pb)
        return hf, pf, hb, pb

    zero = jnp.zeros((SUBLANES, LANES), F32)
    one = jnp.ones((SUBLANES, LANES), F32)
    hf, pf, hb, pb = lax.fori_loop(0, pitch // LRU_SCAN_UNROLL, scan_body, (zero, one, zero, one))

    row = lax.broadcasted_iota(jnp.int32, (SUBLANES, LANES), 0)
    cf = zero
    c = jnp.zeros((1, LANES), F32)
    for s in range(1, SUBLANES):
        c = pf[s - 1:s] * c + hf[s - 1:s]
        cf = jnp.where(row == s, c, cf)
    cbk = zero
    c = jnp.zeros((1, LANES), F32)
    for s in range(SUBLANES - 2, -1, -1):
        c = pb[s + 1:s + 2] * c + hb[s + 1:s + 2]
        cbk = jnp.where(row == s, c, cbk)
    carry_ref[0:SUBLANES, :] = cf
    carry_ref[SUBLANES:2 * SUBLANES, :] = cbk

    to = min(LRU_CHUNK, seq // SUBLANES)
    rows = lax.broadcasted_iota(jnp.int32, (to, LANES), 0)

    def out_body(c, carry):
        t0 = pl.multiple_of(c * to, to)
        s0 = lax.div(t0, pitch)
        s1 = jnp.minimum(s0 + 1, SUBLANES - 1)
        in_first = rows < (s0 + 1) * pitch - t0
        cf_rows = jnp.where(in_first, carry_ref[pl.ds(s0, 1), :], carry_ref[pl.ds(s1, 1), :])
        cb_rows = jnp.where(in_first, carry_ref[pl.ds(SUBLANES + s0, 1), :],
                            carry_ref[pl.ds(SUBLANES + s1, 1), :])
        sl = pl.ds(t0, to)
        h = bf_ref[sl, :] + af_ref[sl, :] * cf_rows + bb_ref[sl, :] + ab_ref[sl, :] * cb_rows
        o_ref[sl, :] = h * yg_ref[sl, :]
        return carry

    lax.fori_loop(0, seq // to, out_body, 0)


def _lru(xr, yg, conv_w, conv_b, w_cat, b_cat, lam):
    _, b, seq, bw = xr.shape
    lw = LRU_BLOCKS * bw
    pitch = seq // SUBLANES
    while pitch % (2 * LRU_SCAN_UNROLL) != LRU_SCAN_UNROLL:
        pitch += 1
    seq_blk = pl.BlockSpec((None, None, seq, bw), lambda bi, n: (n, bi, 0, 0))
    scan_rows = SUBLANES * pitch
    return pl.pallas_call(
        functools.partial(_lru_kernel, seq=seq, pitch=pitch),
        grid=(b, LRU_BLOCKS),
        in_specs=[
            seq_blk,
            seq_blk,
            pl.BlockSpec((LRU_CONV, bw), lambda bi, n: (0, n)),
            pl.BlockSpec((1, bw), lambda bi, n: (0, n)),
            pl.BlockSpec((1, bw, 4 * bw), lambda bi, n: (n, 0, 0)),
            pl.BlockSpec((1, 1, 4 * bw), lambda bi, n: (n, 0, 0)),
            pl.BlockSpec((1, 2, bw), lambda bi, n: (n, 0, 0)),
        ],
        out_specs=seq_blk,
        out_shape=jax.ShapeDtypeStruct((LRU_BLOCKS, b, seq, bw), F32),
        scratch_shapes=[
            pltpu.VMEM((seq + 2 * SUBLANES, LANES), F32),
            pltpu.VMEM((scan_rows, LANES), F32),
            pltpu.VMEM((scan_rows, LANES), F32),
            pltpu.VMEM((scan_rows, LANES), F32),
            pltpu.VMEM((scan_rows, LANES), F32),
            pltpu.VMEM((2 * SUBLANES, LANES), F32),
        ],
        compiler_params=_cparams(("parallel", "parallel"), 52),
        name="rglru",
    )(xr, yg, conv_w, conv_b.reshape(1, lw), w_cat, b_cat, lam)


def _out_kernel(x_ref, attn_ref, lru_ref, mod_ref, ag_ref, lg_ref, w_ref, n2_ref, o_ref, h_ref):
    attn = jnp.concatenate([attn_ref[c] for c in range(attn_ref.shape[0])], axis=1)
    lru = jnp.concatenate([lru_ref[c] for c in range(lru_ref.shape[0])], axis=1)
    na = _rms(attn, ag_ref[...]).astype(BF16)
    nl = _rms(lru, lg_ref[...]).astype(BF16)
    y = _dot(jnp.concatenate([na, nl], axis=1), w_ref[...])
    mod = mod_ref[0]
    x1 = x_ref[...] + mod[2:3] * y
    o_ref[...] = x1
    h_ref[...] = (_rms(x1, n2_ref[...]) * (1.0 + mod[4:5]) + mod[3:4]).astype(BF16)


def _out_proj(x2, attn3, lru3, mod6, row0, seq, attn_g, lru_g, w_out, layer, norm2_g):
    t, d = x2.shape
    aw = attn3.shape[0] * attn3.shape[2]
    lw = lru3.shape[0] * lru3.shape[2]
    tm = TOKEN_TILE
    tiles_per_seq = seq // tm
    tok = lambda w: pl.BlockSpec((tm, w), lambda i: (i, 0))
    blocked = lambda a: pl.BlockSpec((a.shape[0], tm, a.shape[2]), lambda i: (0, i, 0))
    const = lambda shape: pl.BlockSpec(shape, lambda i: (0,) * len(shape))
    return pl.pallas_call(
        _out_kernel,
        grid=(t // tm,),
        in_specs=[
            tok(d), blocked(attn3), blocked(lru3),
            pl.BlockSpec((1, 6, d), lambda i: (row0 + i // tiles_per_seq, 0, 0)),
            const((1, aw)), const((1, lw)),
            pl.BlockSpec((None,) + w_out.shape[1:], lambda i: (layer, 0, 0)),
            const((1, d)),
        ],
        out_specs=[tok(d), tok(d)],
        out_shape=[jax.ShapeDtypeStruct((t, d), F32), jax.ShapeDtypeStruct((t, d), BF16)],
        compiler_params=_cparams(("parallel",), 52),
        name="out_proj",
    )(x2, attn3, lru3, mod6, attn_g.reshape(1, aw), lru_g.reshape(1, lw), w_out, norm2_g.reshape(1, d))


def _halo_kernel(h_ref, w_ref, o_ref):
    o_ref[...] = _dot(h_ref[...], w_ref[...])


def _ffn_halo(h2, seq, tm, w_gu, layer):
    t, d = h2.shape
    f = w_gu.shape[2] // 2
    n_tiles = t // tm
    tiles_per_seq = seq // tm
    tiles = h2.reshape(n_tiles, tm, d)
    zero_row = jnp.zeros((1, d), h2.dtype)
    pos = jnp.arange(n_tiles) % tiles_per_seq
    prev = jnp.concatenate([zero_row, tiles[:-1, tm - 1]], axis=0)
    prev = jnp.where((pos == 0)[:, None], 0, prev)
    nxt = jnp.concatenate([tiles[1:, 0], zero_row], axis=0)
    nxt = jnp.where((pos == tiles_per_seq - 1)[:, None], 0, nxt)
    rows = jnp.stack([prev, nxt], axis=1).reshape(2 * n_tiles, d)
    out = pl.pallas_call(
        _halo_kernel,
        grid=(f // FF_CHUNK,),
        in_specs=[
            pl.BlockSpec((2 * n_tiles, d), lambda j: (0, 0)),
            pl.BlockSpec((None, d, FF_CHUNK), lambda j: (layer, 0, 2 * j)),
        ],
        out_specs=pl.BlockSpec((2 * n_tiles, FF_CHUNK), lambda j: (0, j)),
        out_shape=jax.ShapeDtypeStruct((2 * n_tiles, f), F32),
        compiler_params=_cparams(("arbitrary",), 32),
        name="ffn_halo",
    )(rows, w_gu)
    return out.reshape(n_tiles, 2, f)


def _ffn_kernel(x_ref, h_ref, gh_ref, mod_ref, wgu_ref, cw_ref, cb_ref, wd_ref, o_ref, ge_ref):
    j = pl.program_id(1)
    tm = x_ref.shape[0]
    fc = wd_ref.shape[0]

    @pl.when(j == 0)
    def _():
        o_ref[...] = jnp.zeros_like(o_ref)

    cols = pl.ds(pl.multiple_of(j * fc, fc), fc)
    h = h_ref[...]
    ge_ref[HALO:HALO + tm, :] = _dot(h, wgu_ref[:, :fc])
    ge_ref[HALO - 1:HALO, :] = gh_ref[0, 0:1, cols]
    ge_ref[HALO + tm:HALO + tm + 1, :] = gh_ref[0, 1:2, cols]
    u = _dot(h, wgu_ref[:, fc:])
    cw = cw_ref[:, cols]
    g = ge_ref[HALO - 1:HALO - 1 + tm, :] * cw[0:1] + cb_ref[:, cols]
    g = g + ge_ref[HALO:HALO + tm, :] * cw[1:2]
    g = g + ge_ref[HALO + 1:HALO + 1 + tm, :] * cw[2:3]
    act = (_gelu_tanh(g) * u).astype(BF16)
    o_ref[...] += _dot(act, wd_ref[...])

    @pl.when(j == pl.num_programs(1) - 1)
    def _():
        o_ref[...] = x_ref[...] + mod_ref[0][5:6] * o_ref[...]


def _ffn(x2, h2, mod6, row0, seq, w_gu, conv_w, conv_b, w_down, layer):
    t, d = x2.shape
    f = w_gu.shape[2] // 2
    fc = FF_CHUNK
    tm = min(FFN_TOKEN_TILE, seq)
    tiles_per_seq = seq // tm
    ge_halo = _ffn_halo(h2, seq, tm, w_gu, layer)
    return pl.pallas_call(
        _ffn_kernel,
        grid=(t // tm, f // fc),
        in_specs=[
            pl.BlockSpec((tm, d), lambda i, j: (i, 0)),
            pl.BlockSpec((tm, d), lambda i, j: (i, 0)),
            pl.BlockSpec((1, 2, f), lambda i, j: (i, 0, 0)),
            pl.BlockSpec((1, 6, d), lambda i, j: (row0 + i // tiles_per_seq, 0, 0)),
            pl.BlockSpec((None, d, 2 * fc), lambda i, j: (layer, 0, j)),
            pl.BlockSpec((FFN_CONV, f), lambda i, j: (0, 0)),
            pl.BlockSpec((1, f), lambda i, j: (0, 0)),
            pl.BlockSpec((None, fc, d), lambda i, j: (layer, j, 0)),
        ],
        out_specs=pl.BlockSpec((tm, d), lambda i, j: (i, 0)),
        out_shape=jax.ShapeDtypeStruct((t, d), F32),
        scratch_shapes=[pltpu.VMEM((tm + 2 * HALO, fc), F32)],
        compiler_params=_cparams(("parallel", "arbitrary"), 57),
        name="convglu_ffn",
    )(x2, h2, ge_halo, mod6, w_gu, conv_w, conv_b.reshape(1, f), w_down)


def kernel(x_prompt, x_sample, c_prompt, c_sample, rel_bias, w_mod, b_mod, norm1_g, norm2_g, w_in, q_norm_g, k_norm_g, attn_sink, lru_conv_w, lru_conv_b, lru_w_a, lru_b_a, lru_w_x, lru_b_x, lru_lambda, attn_out_g, lru_out_g, w_out, ffn_w_gate, ffn_w_up, ffn_conv_w, ffn_conv_b, ffn_w_down):
    depth, d, _ = w_in.shape
    n_prompt = c_prompt.shape[0]
    n_req = n_prompt + c_sample.shape[0]
    rows = -(-n_req // SUBLANES) * SUBLANES
    c_all = jnp.concatenate([c_prompt, c_sample, jnp.zeros((rows - n_req, d), F32)], axis=0)
    mod = _modulation(c_all, w_mod, b_mod).reshape(depth, rows, 6, d)
    bias = _bias_table(rel_bias)

    w_in_b = w_in.astype(BF16)
    w_out_b = w_out.astype(BF16)
    by_chunk = lambda w: w.astype(BF16).reshape(depth, d, -1, FF_CHUNK)
    w_gu_b = jnp.concatenate([by_chunk(ffn_w_gate), by_chunk(ffn_w_up)], axis=-1).reshape(depth, d, -1)
    w_down_b = ffn_w_down.astype(BF16)
    w_cat = jnp.concatenate([lru_w_a[:, 0], lru_w_a[:, 1], lru_w_x[:, 0], lru_w_x[:, 1]], axis=-1).astype(BF16)
    bw = LRU_BLOCK_DIM
    b_cat = jnp.concatenate(
        [lru_b_a.reshape(depth, 2, LRU_BLOCKS, bw), lru_b_x.reshape(depth, 2, LRU_BLOCKS, bw)], axis=1)
    b_cat = jnp.transpose(b_cat, (0, 2, 1, 3)).reshape(depth, LRU_BLOCKS, 1, 4 * bw)
    lam = jnp.transpose(lru_lambda.reshape(depth, 2, LRU_BLOCKS, bw), (0, 2, 1, 3))
    sink_rows = jnp.repeat(attn_sink.reshape(depth, N_KV_HEADS, GROUP), WINDOW, axis=-1)[..., None]
    sink_rows = jnp.broadcast_to(sink_rows, (depth, N_KV_HEADS, GROUP * WINDOW, HEAD_DIM))

    def run(x, row0):
        b, seq, _ = x.shape
        x2 = x.reshape(b * seq, d)
        for l in range(depth):
            q, k, v, xr, yg = _in_proj(x2, mod[l], row0, seq, norm1_g[l], w_in_b, l, q_norm_g[l], k_norm_g[l])
            sh = lambda a: a.reshape(a.shape[0], b, seq, a.shape[-1])
            flat = lambda a: a.reshape(a.shape[0], b * seq, a.shape[-1])
            attn = _attention(sh(q), sh(k), sh(v), bias, sink_rows[l])
            lru = _lru(sh(xr), sh(yg), lru_conv_w[l], lru_conv_b[l], w_cat[l], b_cat[l], lam[l])
            x2, h2 = _out_proj(x2, flat(attn), flat(lru), mod[l], row0, seq,
                               attn_out_g[l], lru_out_g[l], w_out_b, l, norm2_g[l])
            x2 = _ffn(x2, h2, mod[l], row0, seq, w_gu_b, ffn_conv_w[l], ffn_conv_b[l], w_down_b, l)
        return x2.reshape(b, seq, d)

    return (run(x_prompt, 0), run(x_sample, n_prompt))
```

```python
import functools
import math

import jax
import jax.numpy as jnp
from jax import lax
from jax.experimental import pallas as pl
from jax.experimental.pallas import tpu as pltpu

F32 = jnp.float32
BF16 = jnp.bfloat16

EPS = 1e-6
HEAD_DIM = 128
N_HEADS = 8
N_KV_HEADS = 2
GROUP = N_HEADS // N_KV_HEADS
WINDOW = 128
N_BUCKETS = 32
LRU_BLOCKS = 8
LRU_BLOCK_DIM = 128
LRU_CONV = 4
LRU_C = 8.0
FFN_CONV = 3
NEG_INF = -1e30
LOG2E = math.log2(math.e)
LN2 = math.log(2.0)
SERIES_LIMIT = 0.01
F32_TINY = float(jnp.finfo(jnp.float32).tiny)

SUBLANES = 8
LANES = 128
MIB = 1024 * 1024

TOKEN_TILE = 512
FFN_TOKEN_TILE = 512
FF_CHUNK = 1024
MOD_CHUNK = 1024
ATTN_Q_TILE = 2048
LRU_CHUNK = 512
LRU_SCAN_UNROLL = 12
LRU_SCAN_BATCH = 6
LRU_LOOKAHEAD = 3
ATTN_UNROLL = 4
HALO = SUBLANES


def _cparams(semantics, vmem_mib):
    return pltpu.CompilerParams(dimension_semantics=semantics, vmem_limit_bytes=vmem_mib * MIB)


def _dot(a, b):
    return jnp.dot(a, b, preferred_element_type=F32)


def _rms(x, gain):
    return x * lax.rsqrt(jnp.mean(x * x, axis=-1, keepdims=True) + EPS) * gain


def _gelu_tanh(x):
    return 0.5 * x * (1.0 + jnp.tanh(math.sqrt(2.0 / math.pi) * (x + 0.044715 * (x * x * x))))


def _sigmoid(x):
    return 1.0 / (1.0 + jnp.exp(-x))


def _one_minus_exp(x, exp_x):
    series = x * (-1.0 + x * (-1 / 2 + x * (-1 / 6)))
    return jnp.where(x > -SERIES_LIMIT, series, 1.0 - exp_x)


def _mod_kernel(c_ref, w_ref, b_ref, o_ref):
    c = c_ref[...]
    cs = (c * _sigmoid(c)).astype(BF16)
    o_ref[0] = _dot(cs, w_ref[0].astype(BF16)) + b_ref[0]


def _modulation(c_all, w_mod, b_mod):
    depth, d, n = w_mod.shape
    rows = c_all.shape[0]
    return pl.pallas_call(
        _mod_kernel,
        grid=(depth, n // MOD_CHUNK),
        in_specs=[
            pl.BlockSpec((rows, d), lambda l, j: (0, 0)),
            pl.BlockSpec((1, d, MOD_CHUNK), lambda l, j: (l, 0, j)),
            pl.BlockSpec((1, 1, MOD_CHUNK), lambda l, j: (l, 0, j)),
        ],
        out_specs=pl.BlockSpec((1, rows, MOD_CHUNK), lambda l, j: (l, 0, j)),
        out_shape=jax.ShapeDtypeStruct((depth, rows, n), F32),
        compiler_params=_cparams(("arbitrary", "arbitrary"), 40),
        name="adaln_modulation",
    )(c_all, w_mod, b_mod.reshape(depth, 1, n))


_BUCKET_THRESHOLDS = (12, 16, 23, 32, 46, 64, 91)


def _bias_kernel(rb_ref, o_ref):
    head = pl.program_id(0)
    qi = lax.broadcasted_iota(jnp.int32, (WINDOW, 3 * WINDOW), 0)
    kj = lax.broadcasted_iota(jnp.int32, (WINDOW, 3 * WINDOW), 1)
    rel = kj - WINDOW - qi
    n = jnp.abs(rel)
    half = N_BUCKETS // 2
    large = jnp.full_like(n, half // 2)
    for t in _BUCKET_THRESHOLDS:
        large = large + jnp.where(n >= t, 1, 0)
    bucket = jnp.where(rel > 0, half, 0) + jnp.where(n < half // 2, n, large)
    bias = jnp.zeros((WINDOW, 3 * WINDOW), F32)
    for b in range(N_BUCKETS):
        bias = jnp.where(bucket == b, rb_ref[b, head], bias)
    o_ref[0] = jnp.where(n <= WINDOW, bias * LOG2E, NEG_INF)


def _bias_table(rel_bias):
    out = pl.pallas_call(
        _bias_kernel,
        grid=(N_HEADS,),
        in_specs=[pl.BlockSpec(memory_space=pltpu.SMEM)],
        out_specs=pl.BlockSpec((1, WINDOW, 3 * WINDOW), lambda h: (h, 0, 0)),
        out_shape=jax.ShapeDtypeStruct((N_HEADS, WINDOW, 3 * WINDOW), F32),
        compiler_params=_cparams(("arbitrary",), 16),
        name="rel_bias_table",
    )(rel_bias)
    return out.reshape(N_KV_HEADS, GROUP * WINDOW, 3 * WINDOW)


def _in_kernel(x_ref, mod_ref, g_ref, w_ref, qg_ref, kg_ref, q_ref, k_ref, v_ref, xr_ref, yg_ref):
    mod = mod_ref[0]
    h = _rms(x_ref[...], g_ref[...] * (1.0 + mod[1:2])) + mod[0:1]
    hb = h.astype(BF16)
    nc = GROUP * HEAD_DIM
    aw = N_KV_HEADS * nc
    kvw = N_KV_HEADS * HEAD_DIM
    lw = LRU_BLOCKS * LRU_BLOCK_DIM
    per = nc // LRU_BLOCK_DIM

    def lru_part(dst_ref, base, act):
        for c in range(LRU_BLOCKS // per):
            z = _dot(hb, w_ref[:, base + c * nc:base + (c + 1) * nc])
            z = z if act is None else act(z)
            for i in range(per):
                dst_ref[c * per + i] = z[:, i * LRU_BLOCK_DIM:(i + 1) * LRU_BLOCK_DIM]

    lru_part(yg_ref, aw + 2 * kvw + lw, _gelu_tanh)
    for c in range(N_KV_HEADS):
        z = _dot(hb, w_ref[:, c * nc:(c + 1) * nc])
        for i in range(GROUP):
            zh = z[:, i * HEAD_DIM:(i + 1) * HEAD_DIM]
            q_ref[c, :, i * HEAD_DIM:(i + 1) * HEAD_DIM] = _rms(zh, qg_ref[...]).astype(BF16)
    z = _dot(hb, w_ref[:, aw:aw + 2 * kvw])
    for i in range(N_KV_HEADS):
        k_ref[i] = _rms(z[:, i * HEAD_DIM:(i + 1) * HEAD_DIM], kg_ref[...]).astype(BF16)
        v_ref[i] = z[:, kvw + i * HEAD_DIM:kvw + (i + 1) * HEAD_DIM].astype(BF16)
    lru_part(xr_ref, aw + 2 * kvw, None)


def _in_proj(x2, mod6, row0, seq, norm_g, w_in, layer, q_g, k_g):
    t, d = x2.shape
    gw = GROUP * HEAD_DIM
    tm = TOKEN_TILE
    tiles_per_seq = seq // tm
    blocked = lambda n, w: pl.BlockSpec((n, tm, w), lambda i: (0, i, 0))
    const = lambda shape: pl.BlockSpec(shape, lambda i: (0,) * len(shape))
    return pl.pallas_call(
        _in_kernel,
        grid=(t // tm,),
        in_specs=[
            pl.BlockSpec((tm, d), lambda i: (i, 0)),
            pl.BlockSpec((1, 6, d), lambda i: (row0 + i // tiles_per_seq, 0, 0)),
            const((1, d)),
            pl.BlockSpec((None,) + w_in.shape[1:], lambda i: (layer, 0, 0)),
            const((1, HEAD_DIM)),
            const((1, HEAD_DIM)),
        ],
        out_specs=[blocked(N_KV_HEADS, gw), blocked(N_KV_HEADS, HEAD_DIM), blocked(N_KV_HEADS, HEAD_DIM),
                   blocked(LRU_BLOCKS, LRU_BLOCK_DIM), blocked(LRU_BLOCKS, LRU_BLOCK_DIM)],
        out_shape=[
            jax.ShapeDtypeStruct((N_KV_HEADS, t, gw), BF16),
            jax.ShapeDtypeStruct((N_KV_HEADS, t, HEAD_DIM), BF16),
            jax.ShapeDtypeStruct((N_KV_HEADS, t, HEAD_DIM), BF16),
            jax.ShapeDtypeStruct((LRU_BLOCKS, t, LRU_BLOCK_DIM), F32),
            jax.ShapeDtypeStruct((LRU_BLOCKS, t, LRU_BLOCK_DIM), F32),
        ],
        compiler_params=_cparams(("parallel",), 52),
        name="in_proj",
    )(x2, mod6, norm_g.reshape(1, d), w_in, q_g.reshape(1, HEAD_DIM), k_g.reshape(1, HEAD_DIM))


def _attn_kernel(q_ref, k_ref, v_ref, bias_ref, sink_ref, o_ref, *, seq):
    nb = seq // WINDOW
    blocks = q_ref.shape[0] // WINDOW
    chunk = pl.program_id(2)
    scale = HEAD_DIM ** -0.5 * LOG2E
    sink = sink_ref[0] * LOG2E
    col = lax.broadcasted_iota(jnp.int32, (1, 3 * WINDOW), 1)
    ones = jnp.ones((3 * WINDOW, HEAD_DIM), BF16)
    rows = GROUP * WINDOW

    def scores(j):
        n = chunk * blocks + j
        r0 = j * WINDOW
        qs = jnp.concatenate(
            [q_ref[pl.ds(r0, WINDOW), g * HEAD_DIM:(g + 1) * HEAD_DIM] for g in range(GROUP)], axis=0)
        starts = [pl.multiple_of(jnp.maximum(n - 1, 0) * WINDOW, WINDOW),
                  pl.multiple_of(n * WINDOW, WINDOW),
                  pl.multiple_of(jnp.minimum(n + 1, nb - 1) * WINDOW, WINDOW)]
        kw = jnp.concatenate([k_ref[pl.ds(s, WINDOW), :] for s in starts], axis=0)
        vw = jnp.concatenate(
            [jnp.concatenate([v_ref[pl.ds(s, WINDOW), :] for s in starts], axis=0), ones], axis=1)
        s = lax.dot_general(qs, kw, (((1,), (1,)), ((), ())), preferred_element_type=F32)
        return n, r0, s, vw

    def softmax(n, r0, s, vw):
        s = s * scale + bias_ref[0]
        valid = ((col >= WINDOW) | (n > 0)) & ((col < 2 * WINDOW) | (n < nb - 1))
        s = jnp.where(valid, s, NEG_INF)
        m = jnp.maximum(jnp.broadcast_to(jnp.max(s, axis=-1, keepdims=True), (rows, HEAD_DIM)), sink)
        p = jnp.exp2(s - jnp.concatenate([m, m, m], axis=1)).astype(BF16)
        return r0, p, jnp.exp2(sink - m), vw

    def finish(r0, p, sink_term, vw):
        ov = _dot(p, vw)
        o = ov[:, :HEAD_DIM] / (ov[:, HEAD_DIM:] + sink_term)
        for g in range(GROUP):
            o_ref[pl.ds(r0, WINDOW), g * HEAD_DIM:(g + 1) * HEAD_DIM] = o[g * WINDOW:(g + 1) * WINDOW]

    groups = [range(g0, g0 + ATTN_UNROLL) for g0 in range(0, blocks, ATTN_UNROLL)]
    stage = [scores(j) for j in groups[0]]
    for g in range(len(groups)):
        ahead = [scores(j) for j in groups[g + 1]] if g + 1 < len(groups) else None
        for st in [softmax(*st) for st in stage]:
            finish(*st)
        stage = ahead


def _attention(q, k, v, bias, sink_rows):
    _, b, seq, gw = q.shape
    tq = min(ATTN_Q_TILE, seq)
    seq_blk = pl.BlockSpec((None, None, seq, HEAD_DIM), lambda bi, h, c: (h, bi, 0, 0))
    return pl.pallas_call(
        functools.partial(_attn_kernel, seq=seq),
        grid=(b, N_KV_HEADS, seq // tq),
        in_specs=[
            pl.BlockSpec((None, None, tq, gw), lambda bi, h, c: (h, bi, c, 0)),
            seq_blk,
            seq_blk,
            pl.BlockSpec((1, GROUP * WINDOW, 3 * WINDOW), lambda bi, h, c: (h, 0, 0)),
            pl.BlockSpec((1, GROUP * WINDOW, HEAD_DIM), lambda bi, h, c: (h, 0, 0)),
        ],
        out_specs=pl.BlockSpec((None, None, tq, gw), lambda bi, h, c: (h, bi, c, 0)),
        out_shape=jax.ShapeDtypeStruct((N_KV_HEADS, b, seq, gw), F32),
        compiler_params=_cparams(("parallel", "parallel", "arbitrary"), 40),
        name="window_attention",
    )(q, k, v, bias, sink_rows)


def _lru_kernel(xr_ref, yg_ref, cw_ref, cb_ref, w_ref, gb_ref, lam_ref, o_ref,
                xp_ref, af_ref, bf_ref, ab_ref, bb_ref, carry_ref, *, seq, pitch):
    tc = min(LRU_CHUNK, seq)
    n_chunks = seq // tc
    pad = SUBLANES
    zeros8 = jnp.zeros((pad, LANES), F32)
    xp_ref[0:pad, :] = zeros8
    xp_ref[seq + pad:seq + 2 * pad, :] = zeros8
    tail = SUBLANES * pitch - seq
    for r in (af_ref, bf_ref, ab_ref, bb_ref):
        r[seq:seq + tail, :] = jnp.zeros((tail, LANES), F32)

    def copy_body(c, carry):
        t0 = pl.multiple_of(c * tc, tc)
        xp_ref[pl.ds(t0 + pad, tc), :] = xr_ref[pl.ds(t0, tc), :]
        return carry

    lax.fori_loop(0, n_chunks, copy_body, 0)

    half_log2_decay = (-0.5 * LRU_C * LOG2E) * jax.nn.softplus(-lam_ref[0])
    cw = 0.5 * cw_ref[...]
    cb = 0.5 * cb_ref[...]
    gb = 0.5 * gb_ref[0]
    bw = LRU_BLOCK_DIM

    def gate_body(c, carry):
        t0 = pl.multiple_of(c * tc, tc)
        xh = xp_ref[pl.ds(t0 + pad - 2, tc), :] * cw[0:1] + cb
        for kk in range(1, LRU_CONV):
            xh = xh + xp_ref[pl.ds(t0 + pad - 2 + kk, tc), :] * cw[kk:kk + 1]
        t = jnp.tanh(_dot(xh.astype(BF16), w_ref[0]) + gb)
        for d, (a_ref, b_ref) in enumerate(((af_ref, bf_ref), (ab_ref, bb_ref))):
            log2_a = (1.0 + t[:, d * bw:(d + 1) * bw]) * half_log2_decay[d:d + 1]
            a = jnp.exp2(log2_a)
            a_ref[pl.ds(t0, tc), :] = a
            y = _one_minus_exp(log2_a * (2.0 * LN2), a * a)
            root = y * lax.rsqrt(jnp.maximum(y, F32_TINY))
            b_ref[pl.ds(t0, tc), :] = root * (1.0 + t[:, (2 + d) * bw:(3 + d) * bw]) * xh
        return carry

    lax.fori_loop(0, n_chunks, gate_body, 0)

    def seg(i):
        return pl.ds(i, SUBLANES, stride=pitch)

    def scan_steps(a_ref, b_ref, steps, h, p):
        ab = [(a_ref[seg(i), :], b_ref[seg(i), :]) for i in steps]
        for g0 in range(0, len(steps), LRU_LOOKAHEAD):
            a_run = b_run = None
            for i, (a, b) in zip(steps[g0:g0 + LRU_LOOKAHEAD], ab[g0:g0 + LRU_LOOKAHEAD]):
                a_run, b_run = (a, b) if a_run is None else (a * a_run, a * b_run + b)
                h_i = a_run * h + b_run
                p_i = a_run * p
                b_ref[seg(i), :] = h_i
                a_ref[seg(i), :] = p_i
            h, p = h_i, p_i
        return h, p

    def scan_body(it, carry):
        hf, pf, hb, pb = carry
        for k0 in range(0, LRU_SCAN_UNROLL, LRU_SCAN_BATCH):
            i0 = it * LRU_SCAN_UNROLL + k0
            hf, pf = scan_steps(af_ref, bf_ref, [i0 + k for k in range(LRU_SCAN_BATCH)], hf, pf)
            hb, pb = scan_steps(ab_ref, bb_ref, [pitch - 1 - i0 - k for k in range(LRU_SCAN_BATCH)], hb, pb)
        return hf, pf, hb, pb

    zero = jnp.zeros((SUBLANES, LANES), F32)
    one = jnp.ones((SUBLANES, LANES), F32)
    hf, pf, hb, pb = lax.fori_loop(0, pitch // LRU_SCAN_UNROLL, scan_body, (zero, one, zero, one))

    row = lax.broadcasted_iota(jnp.int32, (SUBLANES, LANES), 0)
    cf = zero
    c = jnp.zeros((1, LANES), F32)
    for s in range(1, SUBLANES):
        c = pf[s - 1:s] * c + hf[s - 1:s]
        cf = jnp.where(row == s, c, cf)
    cbk = zero
    c = jnp.zeros((1, LANES), F32)
    for s in range(SUBLANES - 2, -1, -1):
        c = pb[s + 1:s + 2] * c + hb[s + 1:s + 2]
        cbk = jnp.where(row == s, c, cbk)
    carry_ref[0:SUBLANES, :] = cf
    carry_ref[SUBLANES:2 * SUBLANES, :] = cbk

    to = min(LRU_CHUNK, seq // SUBLANES)
    rows = lax.broadcasted_iota(jnp.int32, (to, LANES), 0)

    def out_body(c, carry):
        t0 = pl.multiple_of(c * to, to)
        s0 = lax.div(t0, pitch)
        s1 = jnp.minimum(s0 + 1, SUBLANES - 1)
        in_first = rows < (s0 + 1) * pitch - t0
        cf_rows = jnp.where(in_first, carry_ref[pl.ds(s0, 1), :], carry_ref[pl.ds(s1, 1), :])
        cb_rows = jnp.where(in_first, carry_ref[pl.ds(SUBLANES + s0, 1), :],
                            carry_ref[pl.ds(SUBLANES + s1, 1), :])
        sl = pl.ds(t0, to)
        h = bf_ref[sl, :] + af_ref[sl, :] * cf_rows + bb_ref[sl, :] + ab_ref[sl, :] * cb_rows
        o_ref[sl, :] = h * yg_ref[sl, :]
        return carry

    lax.fori_loop(0, seq // to, out_body, 0)


def _lru(xr, yg, conv_w, conv_b, w_cat, b_cat, lam):
    _, b, seq, bw = xr.shape
    lw = LRU_BLOCKS * bw
    pitch = seq // SUBLANES
    while pitch % (2 * LRU_SCAN_UNROLL) != LRU_SCAN_UNROLL:
        pitch += 1
    seq_blk = pl.BlockSpec((None, None, seq, bw), lambda bi, n: (n, bi, 0, 0))
    scan_rows = SUBLANES * pitch
    return pl.pallas_call(
        functools.partial(_lru_kernel, seq=seq, pitch=pitch),
        grid=(b, LRU_BLOCKS),
        in_specs=[
            seq_blk,
            seq_blk,
            pl.BlockSpec((LRU_CONV, bw), lambda bi, n: (0, n)),
            pl.BlockSpec((1, bw), lambda bi, n: (0, n)),
            pl.BlockSpec((1, bw, 4 * bw), lambda bi, n: (n, 0, 0)),
            pl.BlockSpec((1, 1, 4 * bw), lambda bi, n: (n, 0, 0)),
            pl.BlockSpec((1, 2, bw), lambda bi, n: (n, 0, 0)),
        ],
        out_specs=seq_blk,
        out_shape=jax.ShapeDtypeStruct((LRU_BLOCKS, b, seq, bw), F32),
        scratch_shapes=[
            pltpu.VMEM((seq + 2 * SUBLANES, LANES), F32),
            pltpu.VMEM((scan_rows, LANES), F32),
            pltpu.VMEM((scan_rows, LANES), F32),
            pltpu.VMEM((scan_rows, LANES), F32),
            pltpu.VMEM((scan_rows, LANES), F32),
            pltpu.VMEM((2 * SUBLANES, LANES), F32),
        ],
        compiler_params=_cparams(("parallel", "parallel"), 52),
        name="rglru",
    )(xr, yg, conv_w, conv_b.reshape(1, lw), w_cat, b_cat, lam)


def _out_kernel(x_ref, attn_ref, lru_ref, mod_ref, ag_ref, lg_ref, w_ref, n2_ref, o_ref, h_ref):
    attn = jnp.concatenate([attn_ref[c] for c in range(attn_ref.shape[0])], axis=1)
    lru = jnp.concatenate([lru_ref[c] for c in range(lru_ref.shape[0])], axis=1)
    na = _rms(attn, ag_ref[...]).astype(BF16)
    nl = _rms(lru, lg_ref[...]).astype(BF16)
    y = _dot(jnp.concatenate([na, nl], axis=1), w_ref[...])
    mod = mod_ref[0]
    x1 = x_ref[...] + mod[2:3] * y
    o_ref[...] = x1
    h_ref[...] = (_rms(x1, n2_ref[...] * (1.0 + mod[4:5])) + mod[3:4]).astype(BF16)


def _out_proj(x2, attn3, lru3, mod6, row0, seq, attn_g, lru_g, w_out, layer, norm2_g):
    t, d = x2.shape
    aw = attn3.shape[0] * attn3.shape[2]
    lw = lru3.shape[0] * lru3.shape[2]
    tm = TOKEN_TILE
    tiles_per_seq = seq // tm
    tok = lambda w: pl.BlockSpec((tm, w), lambda i: (i, 0))
    blocked = lambda a: pl.BlockSpec((a.shape[0], tm, a.shape[2]), lambda i: (0, i, 0))
    const = lambda shape: pl.BlockSpec(shape, lambda i: (0,) * len(shape))
    return pl.pallas_call(
        _out_kernel,
        grid=(t // tm,),
        in_specs=[
            tok(d), blocked(attn3), blocked(lru3),
            pl.BlockSpec((1, 6, d), lambda i: (row0 + i // tiles_per_seq, 0, 0)),
            const((1, aw)), const((1, lw)),
            pl.BlockSpec((None,) + w_out.shape[1:], lambda i: (layer, 0, 0)),
            const((1, d)),
        ],
        out_specs=[tok(d), tok(d)],
        out_shape=[jax.ShapeDtypeStruct((t, d), F32), jax.ShapeDtypeStruct((t, d), BF16)],
        compiler_params=_cparams(("parallel",), 52),
        name="out_proj",
    )(x2, attn3, lru3, mod6, attn_g.reshape(1, aw), lru_g.reshape(1, lw), w_out, norm2_g.reshape(1, d))


def _halo_kernel(h_ref, w_ref, o_ref):
    o_ref[...] = _dot(h_ref[...], w_ref[...])


def _ffn_halo(h2, seq, tm, w_gate, layer):
    t, d = h2.shape
    f = w_gate.shape[2]
    n_tiles = t // tm
    tiles_per_seq = seq // tm
    tiles = h2.reshape(n_tiles, tm, d)
    zero_row = jnp.zeros((1, d), h2.dtype)
    pos = jnp.arange(n_tiles) % tiles_per_seq
    prev = jnp.concatenate([zero_row, tiles[:-1, tm - 1]], axis=0)
    prev = jnp.where((pos == 0)[:, None], 0, prev)
    nxt = jnp.concatenate([tiles[1:, 0], zero_row], axis=0)
    nxt = jnp.where((pos == tiles_per_seq - 1)[:, None], 0, nxt)
    rows = jnp.stack([prev, nxt], axis=1).reshape(2 * n_tiles, d)
    out = pl.pallas_call(
        _halo_kernel,
        grid=(f // FF_CHUNK,),
        in_specs=[
            pl.BlockSpec((2 * n_tiles, d), lambda j: (0, 0)),
            pl.BlockSpec((None, d, FF_CHUNK), lambda j: (layer, 0, j)),
        ],
        out_specs=pl.BlockSpec((2 * n_tiles, FF_CHUNK), lambda j: (0, j)),
        out_shape=jax.ShapeDtypeStruct((2 * n_tiles, f), F32),
        compiler_params=_cparams(("arbitrary",), 32),
        name="ffn_halo",
    )(rows, w_gate)
    return out.reshape(n_tiles, 2, f)


def _ffn_kernel(x_ref, h_ref, gh_ref, mod_ref, wg_ref, wu_ref, cw_ref, cb_ref, wd_ref, o_ref, ge_ref):
    j = pl.program_id(1)
    tm = x_ref.shape[0]
    fc = wg_ref.shape[1]

    @pl.when(j == 0)
    def _():
        o_ref[...] = jnp.zeros_like(o_ref)

    cols = pl.ds(pl.multiple_of(j * fc, fc), fc)
    h = h_ref[...]
    ge_ref[HALO:HALO + tm, :] = _dot(h, wg_ref[...])
    ge_ref[HALO - 1:HALO, :] = gh_ref[0, 0:1, cols]
    ge_ref[HALO + tm:HALO + tm + 1, :] = gh_ref[0, 1:2, cols]
    u = _dot(h, wu_ref[...])
    cw = cw_ref[:, cols]
    g = ge_ref[HALO - 1:HALO - 1 + tm, :] * cw[0:1] + cb_ref[:, cols]
    g = g + ge_ref[HALO:HALO + tm, :] * cw[1:2]
    g = g + ge_ref[HALO + 1:HALO + 1 + tm, :] * cw[2:3]
    act = (_gelu_tanh(g) * u).astype(BF16)
    o_ref[...] += _dot(act, wd_ref[...])

    @pl.when(j == pl.num_programs(1) - 1)
    def _():
        o_ref[...] = x_ref[...] + mod_ref[0][5:6] * o_ref[...]


def _ffn(x2, h2, mod6, row0, seq, w_gate, w_up, conv_w, conv_b, w_down, layer):
    t, d = x2.shape
    f = w_gate.shape[2]
    fc = FF_CHUNK
    tm = min(FFN_TOKEN_TILE, seq)
    tiles_per_seq = seq // tm
    ge_halo = _ffn_halo(h2, seq, tm, w_gate, layer)
    return pl.pallas_call(
        _ffn_kernel,
        grid=(t // tm, f // fc),
        in_specs=[
            pl.BlockSpec((tm, d), lambda i, j: (i, 0)),
            pl.BlockSpec((tm, d), lambda i, j: (i, 0)),
            pl.BlockSpec((1, 2, f), lambda i, j: (i, 0, 0)),
            pl.BlockSpec((1, 6, d), lambda i, j: (row0 + i // tiles_per_seq, 0, 0)),
            pl.BlockSpec((None, d, fc), lambda i, j: (layer, 0, j)),
            pl.BlockSpec((None, d, fc), lambda i, j: (layer, 0, j)),
            pl.BlockSpec((FFN_CONV, f), lambda i, j: (0, 0)),
            pl.BlockSpec((1, f), lambda i, j: (0, 0)),
            pl.BlockSpec((None, fc, d), lambda i, j: (layer, j, 0)),
        ],
        out_specs=pl.BlockSpec((tm, d), lambda i, j: (i, 0)),
        out_shape=jax.ShapeDtypeStruct((t, d), F32),
        scratch_shapes=[pltpu.VMEM((tm + 2 * HALO, fc), F32)],
        compiler_params=_cparams(("parallel", "arbitrary"), 57),
        name="convglu_ffn",
    )(x2, h2, ge_halo, mod6, w_gate, w_up, conv_w, conv_b.reshape(1, f), w_down)


def kernel(x_prompt, x_sample, c_prompt, c_sample, rel_bias, w_mod, b_mod, norm1_g, norm2_g, w_in, q_norm_g, k_norm_g, attn_sink, lru_conv_w, lru_conv_b, lru_w_a, lru_b_a, lru_w_x, lru_b_x, lru_lambda, attn_out_g, lru_out_g, w_out, ffn_w_gate, ffn_w_up, ffn_conv_w, ffn_conv_b, ffn_w_down):
    depth, d, _ = w_in.shape
    n_prompt = c_prompt.shape[0]
    n_req = n_prompt + c_sample.shape[0]
    rows = -(-n_req // SUBLANES) * SUBLANES
    c_all = jnp.concatenate([c_prompt, c_sample, jnp.zeros((rows - n_req, d), F32)], axis=0)
    mod = _modulation(c_all, w_mod, b_mod).reshape(depth, rows, 6, d)
    bias = _bias_table(rel_bias)

    w_in_b = w_in.astype(BF16)
    w_out_b = w_out.astype(BF16)
    w_gate_b = ffn_w_gate.astype(BF16)
    w_up_b = ffn_w_up.astype(BF16)
    w_down_b = ffn_w_down.astype(BF16)
    w_cat = jnp.concatenate([lru_w_a[:, 0], lru_w_a[:, 1], lru_w_x[:, 0], lru_w_x[:, 1]], axis=-1).astype(BF16)
    bw = LRU_BLOCK_DIM
    b_cat = jnp.concatenate(
        [lru_b_a.reshape(depth, 2, LRU_BLOCKS, bw), lru_b_x.reshape(depth, 2, LRU_BLOCKS, bw)], axis=1)
    b_cat = jnp.transpose(b_cat, (0, 2, 1, 3)).reshape(depth, LRU_BLOCKS, 1, 4 * bw)
    lam = jnp.transpose(lru_lambda.reshape(depth, 2, LRU_BLOCKS, bw), (0, 2, 1, 3))
    sink_rows = jnp.repeat(attn_sink.reshape(depth, N_KV_HEADS, GROUP), WINDOW, axis=-1)[..., None]
    sink_rows = jnp.broadcast_to(sink_rows, (depth, N_KV_HEADS, GROUP * WINDOW, HEAD_DIM))

    def run(x, row0):
        b, seq, _ = x.shape
        x2 = x.reshape(b * seq, d)
        for l in range(depth):
            q, k, v, xr, yg = _in_proj(x2, mod[l], row0, seq, norm1_g[l], w_in_b, l, q_norm_g[l], k_norm_g[l])
            sh = lambda a: a.reshape(a.shape[0], b, seq, a.shape[-1])
            flat = lambda a: a.reshape(a.shape[0], b * seq, a.shape[-1])
            attn = _attention(sh(q), sh(k), sh(v), bias, sink_rows[l])
            lru = _lru(sh(xr), sh(yg), lru_conv_w[l], lru_conv_b[l], w_cat[l], b_cat[l], lam[l])
            x2, h2 = _out_proj(x2, flat(attn), flat(lru), mod[l], row0, seq,
                               attn_out_g[l], lru_out_g[l], w_out_b, l, norm2_g[l])
            x2 = _ffn(x2, h2, mod[l], row0, seq, w_gate_b, w_up_b, ffn_conv_w[l], ffn_conv_b[l], w_down_b, l)
        return x2.reshape(b, seq, d)

    return (run(x_prompt, 0), run(x_sample, n_prompt))
```

```python
import functools
import math

import jax
import jax.numpy as jnp
from jax import lax
from jax.experimental import pallas as pl
from jax.experimental.pallas import tpu as pltpu

F32 = jnp.float32
BF16 = jnp.bfloat16

EPS = 1e-6
HEAD_DIM = 128
N_HEADS = 8
N_KV_HEADS = 2
GROUP = N_HEADS // N_KV_HEADS
WINDOW = 128
N_BUCKETS = 32
LRU_BLOCKS = 8
LRU_BLOCK_DIM = 128
LRU_CONV = 4
LRU_C = 8.0
FFN_CONV = 3
NEG_INF = -1e30
LOG2E = math.log2(math.e)
LN2 = math.log(2.0)
SERIES_LIMIT = 0.01
F32_TINY = float(jnp.finfo(jnp.float32).tiny)

SUBLANES = 8
LANES = 128
MIB = 1024 * 1024

TOKEN_TILE = 512
FFN_TOKEN_TILE = 512
FF_CHUNK = 1024
MOD_CHUNK = 1024
ATTN_Q_TILE = 2048
LRU_CHUNK = 1024
LRU_SCAN_UNROLL = 12
LRU_SCAN_BATCH = 6
LRU_LOOKAHEAD = 3
ATTN_UNROLL = 4
HALO = SUBLANES


def _cparams(semantics, vmem_mib):
    return pltpu.CompilerParams(dimension_semantics=semantics, vmem_limit_bytes=vmem_mib * MIB)


def _dot(a, b):
    return jnp.dot(a, b, preferred_element_type=F32)


def _rms(x, gain):
    return x * lax.rsqrt(jnp.mean(x * x, axis=-1, keepdims=True) + EPS) * gain


def _gelu_tanh(x):
    return 0.5 * x * (1.0 + jnp.tanh(math.sqrt(2.0 / math.pi) * (x + 0.044715 * (x * x * x))))


def _sigmoid(x):
    return 1.0 / (1.0 + jnp.exp(-x))


def _one_minus_exp(x, exp_x):
    series = x * (-1.0 + x * (-1 / 2 + x * (-1 / 6)))
    return jnp.where(x > -SERIES_LIMIT, series, 1.0 - exp_x)


def _mod_kernel(c_ref, w_ref, b_ref, o_ref):
    c = c_ref[...]
    cs = (c * _sigmoid(c)).astype(BF16)
    o_ref[0] = _dot(cs, w_ref[0].astype(BF16)) + b_ref[0]


def _modulation(c_all, w_mod, b_mod):
    depth, d, n = w_mod.shape
    rows = c_all.shape[0]
    return pl.pallas_call(
        _mod_kernel,
        grid=(depth, n // MOD_CHUNK),
        in_specs=[
            pl.BlockSpec((rows, d), lambda l, j: (0, 0)),
            pl.BlockSpec((1, d, MOD_CHUNK), lambda l, j: (l, 0, j)),
            pl.BlockSpec((1, 1, MOD_CHUNK), lambda l, j: (l, 0, j)),
        ],
        out_specs=pl.BlockSpec((1, rows, MOD_CHUNK), lambda l, j: (l, 0, j)),
        out_shape=jax.ShapeDtypeStruct((depth, rows, n), F32),
        compiler_params=_cparams(("arbitrary", "arbitrary"), 40),
        name="adaln_modulation",
    )(c_all, w_mod, b_mod.reshape(depth, 1, n))


_BUCKET_THRESHOLDS = (12, 16, 23, 32, 46, 64, 91)


def _bias_kernel(rb_ref, o_ref):
    head = pl.program_id(0)
    qi = lax.broadcasted_iota(jnp.int32, (WINDOW, 3 * WINDOW), 0)
    kj = lax.broadcasted_iota(jnp.int32, (WINDOW, 3 * WINDOW), 1)
    rel = kj - WINDOW - qi
    n = jnp.abs(rel)
    half = N_BUCKETS // 2
    large = jnp.full_like(n, half // 2)
    for t in _BUCKET_THRESHOLDS:
        large = large + jnp.where(n >= t, 1, 0)
    bucket = jnp.where(rel > 0, half, 0) + jnp.where(n < half // 2, n, large)
    bias = jnp.zeros((WINDOW, 3 * WINDOW), F32)
    for b in range(N_BUCKETS):
        bias = jnp.where(bucket == b, rb_ref[b, head], bias)
    o_ref[0] = jnp.where(n <= WINDOW, bias * LOG2E, NEG_INF)


def _bias_table(rel_bias):
    out = pl.pallas_call(
        _bias_kernel,
        grid=(N_HEADS,),
        in_specs=[pl.BlockSpec(memory_space=pltpu.SMEM)],
        out_specs=pl.BlockSpec((1, WINDOW, 3 * WINDOW), lambda h: (h, 0, 0)),
        out_shape=jax.ShapeDtypeStruct((N_HEADS, WINDOW, 3 * WINDOW), F32),
        compiler_params=_cparams(("arbitrary",), 16),
        name="rel_bias_table",
    )(rel_bias)
    return out.reshape(N_KV_HEADS, GROUP * WINDOW, 3 * WINDOW)


def _in_kernel(x_ref, mod_ref, g_ref, w_ref, qg_ref, kg_ref, q_ref, k_ref, v_ref, xr_ref, yg_ref):
    mod = mod_ref[0]
    h = _rms(x_ref[...], g_ref[...] * (1.0 + mod[1:2])) + mod[0:1]
    hb = h.astype(BF16)
    nc = GROUP * HEAD_DIM
    aw = N_KV_HEADS * nc
    kvw = N_KV_HEADS * HEAD_DIM
    lw = LRU_BLOCKS * LRU_BLOCK_DIM
    per = nc // LRU_BLOCK_DIM

    def lru_part(dst_ref, base, act):
        for c in range(LRU_BLOCKS // per):
            z = _dot(hb, w_ref[:, base + c * nc:base + (c + 1) * nc])
            z = z if act is None else act(z)
            for i in range(per):
                dst_ref[c * per + i] = z[:, i * LRU_BLOCK_DIM:(i + 1) * LRU_BLOCK_DIM]

    lru_part(yg_ref, aw + 2 * kvw + lw, _gelu_tanh)
    for c in range(N_KV_HEADS):
        z = _dot(hb, w_ref[:, c * nc:(c + 1) * nc])
        for i in range(GROUP):
            zh = z[:, i * HEAD_DIM:(i + 1) * HEAD_DIM]
            q_ref[c, :, i * HEAD_DIM:(i + 1) * HEAD_DIM] = _rms(zh, qg_ref[...]).astype(BF16)
    z = _dot(hb, w_ref[:, aw:aw + 2 * kvw])
    for i in range(N_KV_HEADS):
        k_ref[i] = _rms(z[:, i * HEAD_DIM:(i + 1) * HEAD_DIM], kg_ref[...]).astype(BF16)
        v_ref[i] = z[:, kvw + i * HEAD_DIM:kvw + (i + 1) * HEAD_DIM].astype(BF16)
    lru_part(xr_ref, aw + 2 * kvw, None)


def _in_proj(x2, mod6, row0, seq, norm_g, w_in, layer, q_g, k_g):
    t, d = x2.shape
    gw = GROUP * HEAD_DIM
    tm = TOKEN_TILE
    tiles_per_seq = seq // tm
    blocked = lambda n, w: pl.BlockSpec((n, tm, w), lambda i: (0, i, 0))
    const = lambda shape: pl.BlockSpec(shape, lambda i: (0,) * len(shape))
    return pl.pallas_call(
        _in_kernel,
        grid=(t // tm,),
        in_specs=[
            pl.BlockSpec((tm, d), lambda i: (i, 0)),
            pl.BlockSpec((1, 6, d), lambda i: (row0 + i // tiles_per_seq, 0, 0)),
            const((1, d)),
            pl.BlockSpec((None,) + w_in.shape[1:], lambda i: (layer, 0, 0)),
            const((1, HEAD_DIM)),
            const((1, HEAD_DIM)),
        ],
        out_specs=[blocked(N_KV_HEADS, gw), blocked(N_KV_HEADS, HEAD_DIM), blocked(N_KV_HEADS, HEAD_DIM),
                   blocked(LRU_BLOCKS, LRU_BLOCK_DIM), blocked(LRU_BLOCKS, LRU_BLOCK_DIM)],
        out_shape=[
            jax.ShapeDtypeStruct((N_KV_HEADS, t, gw), BF16),
            jax.ShapeDtypeStruct((N_KV_HEADS, t, HEAD_DIM), BF16),
            jax.ShapeDtypeStruct((N_KV_HEADS, t, HEAD_DIM), BF16),
            jax.ShapeDtypeStruct((LRU_BLOCKS, t, LRU_BLOCK_DIM), F32),
            jax.ShapeDtypeStruct((LRU_BLOCKS, t, LRU_BLOCK_DIM), F32),
        ],
        compiler_params=_cparams(("parallel",), 52),
        name="in_proj",
    )(x2, mod6, norm_g.reshape(1, d), w_in, q_g.reshape(1, HEAD_DIM), k_g.reshape(1, HEAD_DIM))


def _attn_kernel(q_ref, k_ref, v_ref, bias_ref, sink_ref, o_ref, *, seq):
    nb = seq // WINDOW
    blocks = q_ref.shape[0] // WINDOW
    chunk = pl.program_id(2)
    scale = HEAD_DIM ** -0.5 * LOG2E
    sink = sink_ref[0] * LOG2E
    col = lax.broadcasted_iota(jnp.int32, (1, 3 * WINDOW), 1)
    ones = jnp.ones((3 * WINDOW, HEAD_DIM), BF16)
    rows = GROUP * WINDOW

    def scores(j):
        n = chunk * blocks + j
        r0 = j * WINDOW
        qs = jnp.concatenate(
            [q_ref[pl.ds(r0, WINDOW), g * HEAD_DIM:(g + 1) * HEAD_DIM] for g in range(GROUP)], axis=0)
        starts = [pl.multiple_of(jnp.maximum(n - 1, 0) * WINDOW, WINDOW),
                  pl.multiple_of(n * WINDOW, WINDOW),
                  pl.multiple_of(jnp.minimum(n + 1, nb - 1) * WINDOW, WINDOW)]
        kw = jnp.concatenate([k_ref[pl.ds(s, WINDOW), :] for s in starts], axis=0)
        vw = jnp.concatenate(
            [jnp.concatenate([v_ref[pl.ds(s, WINDOW), :] for s in starts], axis=0), ones], axis=1)
        s = lax.dot_general(qs, kw, (((1,), (1,)), ((), ())), preferred_element_type=F32)
        return n, r0, s, vw

    def softmax(n, r0, s, vw):
        s = s * scale + bias_ref[0]
        valid = ((col >= WINDOW) | (n > 0)) & ((col < 2 * WINDOW) | (n < nb - 1))
        s = jnp.where(valid, s, NEG_INF)
        m = jnp.maximum(jnp.broadcast_to(jnp.max(s, axis=-1, keepdims=True), (rows, HEAD_DIM)), sink)
        p = jnp.exp2(s - jnp.concatenate([m, m, m], axis=1)).astype(BF16)
        return r0, p, jnp.exp2(sink - m), vw

    def finish(r0, p, sink_term, vw):
        ov = _dot(p, vw)
        o = ov[:, :HEAD_DIM] / (ov[:, HEAD_DIM:] + sink_term)
        for g in range(GROUP):
            o_ref[pl.ds(r0, WINDOW), g * HEAD_DIM:(g + 1) * HEAD_DIM] = o[g * WINDOW:(g + 1) * WINDOW]

    groups = [range(g0, g0 + ATTN_UNROLL) for g0 in range(0, blocks, ATTN_UNROLL)]
    stage = [scores(j) for j in groups[0]]
    for g in range(len(groups)):
        ahead = [scores(j) for j in groups[g + 1]] if g + 1 < len(groups) else None
        for st in [softmax(*st) for st in stage]:
            finish(*st)
        stage = ahead


def _attention(q, k, v, bias, sink_rows):
    _, b, seq, gw = q.shape
    tq = min(ATTN_Q_TILE, seq)
    seq_blk = pl.BlockSpec((None, None, seq, HEAD_DIM), lambda bi, h, c: (h, bi, 0, 0))
    return pl.pallas_call(
        functools.partial(_attn_kernel, seq=seq),
        grid=(b, N_KV_HEADS, seq // tq),
        in_specs=[
            pl.BlockSpec((None, None, tq, gw), lambda bi, h, c: (h, bi, c, 0)),
            seq_blk,
            seq_blk,
            pl.BlockSpec((1, GROUP * WINDOW, 3 * WINDOW), lambda bi, h, c: (h, 0, 0)),
            pl.BlockSpec((1, GROUP * WINDOW, HEAD_DIM), lambda bi, h, c: (h, 0, 0)),
        ],
        out_specs=pl.BlockSpec((None, None, tq, gw), lambda bi, h, c: (h, bi, c, 0)),
        out_shape=jax.ShapeDtypeStruct((N_KV_HEADS, b, seq, gw), F32),
        compiler_params=_cparams(("parallel", "parallel", "arbitrary"), 40),
        name="window_attention",
    )(q, k, v, bias, sink_rows)


def _lru_kernel(xr_ref, yg_ref, cw_ref, cb_ref, w_ref, gb_ref, lam_ref, o_ref,
                xp_ref, af_ref, bf_ref, ab_ref, bb_ref, carry_ref, *, seq, pitch):
    tc = min(LRU_CHUNK, seq)
    n_chunks = seq // tc
    pad = SUBLANES
    zeros8 = jnp.zeros((pad, LANES), F32)
    xp_ref[0:pad, :] = zeros8
    xp_ref[seq + pad:seq + 2 * pad, :] = zeros8
    tail = SUBLANES * pitch - seq
    for r in (af_ref, bf_ref, ab_ref, bb_ref):
        r[seq:seq + tail, :] = jnp.zeros((tail, LANES), F32)

    def copy_body(c, carry):
        t0 = pl.multiple_of(c * tc, tc)
        xp_ref[pl.ds(t0 + pad, tc), :] = xr_ref[pl.ds(t0, tc), :]
        return carry

    lax.fori_loop(0, n_chunks, copy_body, 0)

    half_log2_decay = (-0.5 * LRU_C * LOG2E) * jax.nn.softplus(-lam_ref[0])
    cw = 0.5 * cw_ref[...]
    cb = 0.5 * cb_ref[...]
    gb = 0.5 * gb_ref[0]
    bw = LRU_BLOCK_DIM

    def gate_body(c, carry):
        t0 = pl.multiple_of(c * tc, tc)
        xh = xp_ref[pl.ds(t0 + pad - 2, tc), :] * cw[0:1] + cb
        for kk in range(1, LRU_CONV):
            xh = xh + xp_ref[pl.ds(t0 + pad - 2 + kk, tc), :] * cw[kk:kk + 1]
        t = jnp.tanh(_dot(xh.astype(BF16), w_ref[0]) + gb)
        for d, (a_ref, b_ref) in enumerate(((af_ref, bf_ref), (ab_ref, bb_ref))):
            log2_a = (1.0 + t[:, d * bw:(d + 1) * bw]) * half_log2_decay[d:d + 1]
            a = jnp.exp2(log2_a)
            a_ref[pl.ds(t0, tc), :] = a
            y = _one_minus_exp(log2_a * (2.0 * LN2), a * a)
            root = y * lax.rsqrt(jnp.maximum(y, F32_TINY))
            b_ref[pl.ds(t0, tc), :] = root * (1.0 + t[:, (2 + d) * bw:(3 + d) * bw]) * xh
        return carry

    lax.fori_loop(0, n_chunks, gate_body, 0)

    def seg(i):
        return pl.ds(i, SUBLANES, stride=pitch)

    def scan_steps(a_ref, b_ref, steps, h, p):
        ab = [(a_ref[seg(i), :], b_ref[seg(i), :]) for i in steps]
        for g0 in range(0, len(steps), LRU_LOOKAHEAD):
            a_run = b_run = None
            for i, (a, b) in zip(steps[g0:g0 + LRU_LOOKAHEAD], ab[g0:g0 + LRU_LOOKAHEAD]):
                a_run, b_run = (a, b) if a_run is None else (a * a_run, a * b_run + b)
                h_i = a_run * h + b_run
                p_i = a_run * p
                b_ref[seg(i), :] = h_i
                a_ref[seg(i), :] = p_i
            h, p = h_i, p_i
        return h, p

    def scan_body(it, carry):
        hf, pf, hb, pb = carry
        for k0 in range(0, LRU_SCAN_UNROLL, LRU_SCAN_BATCH):
            i0 = it * LRU_SCAN_UNROLL + k0
            hf, pf = scan_steps(af_ref, bf_ref, [i0 + k for k in range(LRU_SCAN_BATCH)], hf, pf)
            hb, pb = scan_steps(ab_ref, bb_ref, [pitch - 1 - i0 - k for k in range(LRU_SCAN_BATCH)], hb, pb)
        return hf, pf, hb, pb

    zero = jnp.zeros((SUBLANES, LANES), F32)
    one = jnp.ones((SUBLANES, LANES), F32)
    hf, pf, hb, pb = lax.fori_loop(0, pitch // LRU_SCAN_UNROLL, scan_body, (zero, one, zero, one))

    row = lax.broadcasted_iota(jnp.int32, (SUBLANES, LANES), 0)
    cf = zero
    c = jnp.zeros((1, LANES), F32)
    for s in range(1, SUBLANES):
        c = pf[s - 1:s] * c + hf[s - 1:s]
        cf = jnp.where(row == s, c, cf)
    cbk = zero
    c = jnp.zeros((1, LANES), F32)
    for s in range(SUBLANES - 2, -1, -1):
        c = pb[s + 1:s + 2] * c + hb[s + 1:s + 2]
        cbk = jnp.where(row == s, c, cbk)
    carry_ref[0:SUBLANES, :] = cf
    carry_ref[SUBLANES:2 * SUBLANES, :] = cbk

    to = min(LRU_CHUNK, seq // SUBLANES)
    rows = lax.broadcasted_iota(jnp.int32, (to, LANES), 0)

    def out_body(c, carry):
        t0 = pl.multiple_of(c * to, to)
        s0 = lax.div(t0, pitch)
        s1 = jnp.minimum(s0 + 1, SUBLANES - 1)
        in_first = rows < (s0 + 1) * pitch - t0
        cf_rows = jnp.where(in_first, carry_ref[pl.ds(s0, 1), :], carry_ref[pl.ds(s1, 1), :])
        cb_rows = jnp.where(in_first, carry_ref[pl.ds(SUBLANES + s0, 1), :],
                            carry_ref[pl.ds(SUBLANES + s1, 1), :])
        sl = pl.ds(t0, to)
        h = bf_ref[sl, :] + af_ref[sl, :] * cf_rows + bb_ref[sl, :] + ab_ref[sl, :] * cb_rows
        o_ref[sl, :] = h * yg_ref[sl, :]
        return carry

    lax.fori_loop(0, seq // to, out_body, 0)


def _lru(xr, yg, conv_w, conv_b, w_cat, b_cat, lam):
    _, b, seq, bw = xr.shape
    lw = LRU_BLOCKS * bw
    pitch = seq // SUBLANES
    while pitch % (2 * LRU_SCAN_UNROLL) != LRU_SCAN_UNROLL:
        pitch += 1
    seq_blk = pl.BlockSpec((None, None, seq, bw), lambda bi, n: (n, bi, 0, 0))
    scan_rows = SUBLANES * pitch
    return pl.pallas_call(
        functools.partial(_lru_kernel, seq=seq, pitch=pitch),
        grid=(b, LRU_BLOCKS),
        in_specs=[
            seq_blk,
            seq_blk,
            pl.BlockSpec((LRU_CONV, bw), lambda bi, n: (0, n)),
            pl.BlockSpec((1, bw), lambda bi, n: (0, n)),
            pl.BlockSpec((1, bw, 4 * bw), lambda bi, n: (n, 0, 0)),
            pl.BlockSpec((1, 1, 4 * bw), lambda bi, n: (n, 0, 0)),
            pl.BlockSpec((1, 2, bw), lambda bi, n: (n, 0, 0)),
        ],
        out_specs=seq_blk,
        out_shape=jax.ShapeDtypeStruct((LRU_BLOCKS, b, seq, bw), F32),
        scratch_shapes=[
            pltpu.VMEM((seq + 2 * SUBLANES, LANES), F32),
            pltpu.VMEM((scan_rows, LANES), F32),
            pltpu.VMEM((scan_rows, LANES), F32),
            pltpu.VMEM((scan_rows, LANES), F32),
            pltpu.VMEM((scan_rows, LANES), F32),
            pltpu.VMEM((2 * SUBLANES, LANES), F32),
        ],
        compiler_params=_cparams(("parallel", "parallel"), 52),
        name="rglru",
    )(xr, yg, conv_w, conv_b.reshape(1, lw), w_cat, b_cat, lam)


def _out_kernel(x_ref, attn_ref, lru_ref, mod_ref, ag_ref, lg_ref, w_ref, n2_ref, o_ref, h_ref):
    attn = jnp.concatenate([attn_ref[c] for c in range(attn_ref.shape[0])], axis=1)
    lru = jnp.concatenate([lru_ref[c] for c in range(lru_ref.shape[0])], axis=1)
    na = _rms(attn, ag_ref[...]).astype(BF16)
    nl = _rms(lru, lg_ref[...]).astype(BF16)
    y = _dot(jnp.concatenate([na, nl], axis=1), w_ref[...])
    mod = mod_ref[0]
    x1 = x_ref[...] + mod[2:3] * y
    o_ref[...] = x1
    h_ref[...] = (_rms(x1, n2_ref[...] * (1.0 + mod[4:5])) + mod[3:4]).astype(BF16)


def _out_proj(x2, attn3, lru3, mod6, row0, seq, attn_g, lru_g, w_out, layer, norm2_g):
    t, d = x2.shape
    aw = attn3.shape[0] * attn3.shape[2]
    lw = lru3.shape[0] * lru3.shape[2]
    tm = TOKEN_TILE
    tiles_per_seq = seq // tm
    tok = lambda w: pl.BlockSpec((tm, w), lambda i: (i, 0))
    blocked = lambda a: pl.BlockSpec((a.shape[0], tm, a.shape[2]), lambda i: (0, i, 0))
    const = lambda shape: pl.BlockSpec(shape, lambda i: (0,) * len(shape))
    return pl.pallas_call(
        _out_kernel,
        grid=(t // tm,),
        in_specs=[
            tok(d), blocked(attn3), blocked(lru3),
            pl.BlockSpec((1, 6, d), lambda i: (row0 + i // tiles_per_seq, 0, 0)),
            const((1, aw)), const((1, lw)),
            pl.BlockSpec((None,) + w_out.shape[1:], lambda i: (layer, 0, 0)),
            const((1, d)),
        ],
        out_specs=[tok(d), tok(d)],
        out_shape=[jax.ShapeDtypeStruct((t, d), F32), jax.ShapeDtypeStruct((t, d), BF16)],
        compiler_params=_cparams(("parallel",), 52),
        name="out_proj",
    )(x2, attn3, lru3, mod6, attn_g.reshape(1, aw), lru_g.reshape(1, lw), w_out, norm2_g.reshape(1, d))


def _halo_kernel(h_ref, w_ref, o_ref):
    o_ref[...] = _dot(h_ref[...], w_ref[...])


def _ffn_halo(h2, seq, tm, w_gate, layer):
    t, d = h2.shape
    f = w_gate.shape[2]
    n_tiles = t // tm
    tiles_per_seq = seq // tm
    tiles = h2.reshape(n_tiles, tm, d)
    zero_row = jnp.zeros((1, d), h2.dtype)
    pos = jnp.arange(n_tiles) % tiles_per_seq
    prev = jnp.concatenate([zero_row, tiles[:-1, tm - 1]], axis=0)
    prev = jnp.where((pos == 0)[:, None], 0, prev)
    nxt = jnp.concatenate([tiles[1:, 0], zero_row], axis=0)
    nxt = jnp.where((pos == tiles_per_seq - 1)[:, None], 0, nxt)
    rows = jnp.stack([prev, nxt], axis=1).reshape(2 * n_tiles, d)
    out = pl.pallas_call(
        _halo_kernel,
        grid=(f // FF_CHUNK,),
        in_specs=[
            pl.BlockSpec((2 * n_tiles, d), lambda j: (0, 0)),
            pl.BlockSpec((None, d, FF_CHUNK), lambda j: (layer, 0, j)),
        ],
        out_specs=pl.BlockSpec((2 * n_tiles, FF_CHUNK), lambda j: (0, j)),
        out_shape=jax.ShapeDtypeStruct((2 * n_tiles, f), F32),
        compiler_params=_cparams(("arbitrary",), 32),
        name="ffn_halo",
    )(rows, w_gate)
    return out.reshape(n_tiles, 2, f)


def _ffn_kernel(x_ref, h_ref, gh_ref, mod_ref, wg_ref, wu_ref, cw_ref, cb_ref, wd_ref, o_ref, ge_ref):
    j = pl.program_id(1)
    tm = x_ref.shape[0]
    fc = wg_ref.shape[1]

    @pl.when(j == 0)
    def _():
        o_ref[...] = jnp.zeros_like(o_ref)

    cols = pl.ds(pl.multiple_of(j * fc, fc), fc)
    h = h_ref[...]
    ge_ref[HALO:HALO + tm, :] = _dot(h, wg_ref[...])
    ge_ref[HALO - 1:HALO, :] = gh_ref[0, 0:1, cols]
    ge_ref[HALO + tm:HALO + tm + 1, :] = gh_ref[0, 1:2, cols]
    u = _dot(h, wu_ref[...])
    cw = cw_ref[:, cols]
    g = ge_ref[HALO - 1:HALO - 1 + tm, :] * cw[0:1] + cb_ref[:, cols]
    g = g + ge_ref[HALO:HALO + tm, :] * cw[1:2]
    g = g + ge_ref[HALO + 1:HALO + 1 + tm, :] * cw[2:3]
    act = (_gelu_tanh(g) * u).astype(BF16)
    o_ref[...] += _dot(act, wd_ref[...])

    @pl.when(j == pl.num_programs(1) - 1)
    def _():
        o_ref[...] = x_ref[...] + mod_ref[0][5:6] * o_ref[...]


def _ffn(x2, h2, mod6, row0, seq, w_gate, w_up, conv_w, conv_b, w_down, layer):
    t, d = x2.shape
    f = w_gate.shape[2]
    fc = FF_CHUNK
    tm = min(FFN_TOKEN_TILE, seq)
    tiles_per_seq = seq // tm
    ge_halo = _ffn_halo(h2, seq, tm, w_gate, layer)
    return pl.pallas_call(
        _ffn_kernel,
        grid=(t // tm, f // fc),
        in_specs=[
            pl.BlockSpec((tm, d), lambda i, j: (i, 0)),
            pl.BlockSpec((tm, d), lambda i, j: (i, 0)),
            pl.BlockSpec((1, 2, f), lambda i, j: (i, 0, 0)),
            pl.BlockSpec((1, 6, d), lambda i, j: (row0 + i // tiles_per_seq, 0, 0)),
            pl.BlockSpec((None, d, fc), lambda i, j: (layer, 0, j)),
            pl.BlockSpec((None, d, fc), lambda i, j: (layer, 0, j)),
            pl.BlockSpec((FFN_CONV, f), lambda i, j: (0, 0)),
            pl.BlockSpec((1, f), lambda i, j: (0, 0)),
            pl.BlockSpec((None, fc, d), lambda i, j: (layer, j, 0)),
        ],
        out_specs=pl.BlockSpec((tm, d), lambda i, j: (i, 0)),
        out_shape=jax.ShapeDtypeStruct((t, d), F32),
        scratch_shapes=[pltpu.VMEM((tm + 2 * HALO, fc), F32)],
        compiler_params=_cparams(("parallel", "arbitrary"), 57),
        name="convglu_ffn",
    )(x2, h2, ge_halo, mod6, w_gate, w_up, conv_w, conv_b.reshape(1, f), w_down)


def kernel(x_prompt, x_sample, c_prompt, c_sample, rel_bias, w_mod, b_mod, norm1_g, norm2_g, w_in, q_norm_g, k_norm_g, attn_sink, lru_conv_w, lru_conv_b, lru_w_a, lru_b_a, lru_w_x, lru_b_x, lru_lambda, attn_out_g, lru_out_g, w_out, ffn_w_gate, ffn_w_up, ffn_conv_w, ffn_conv_b, ffn_w_down):
    depth, d, _ = w_in.shape
    n_prompt = c_prompt.shape[0]
    n_req = n_prompt + c_sample.shape[0]
    rows = -(-n_req // SUBLANES) * SUBLANES
    c_all = jnp.concatenate([c_prompt, c_sample, jnp.zeros((rows - n_req, d), F32)], axis=0)
    mod = _modulation(c_all, w_mod, b_mod).reshape(depth, rows, 6, d)
    bias = _bias_table(rel_bias)

    w_in_b = w_in.astype(BF16)
    w_out_b = w_out.astype(BF16)
    w_gate_b = ffn_w_gate.astype(BF16)
    w_up_b = ffn_w_up.astype(BF16)
    w_down_b = ffn_w_down.astype(BF16)
    w_cat = jnp.concatenate([lru_w_a[:, 0], lru_w_a[:, 1], lru_w_x[:, 0], lru_w_x[:, 1]], axis=-1).astype(BF16)
    bw = LRU_BLOCK_DIM
    b_cat = jnp.concatenate(
        [lru_b_a.reshape(depth, 2, LRU_BLOCKS, bw), lru_b_x.reshape(depth, 2, LRU_BLOCKS, bw)], axis=1)
    b_cat = jnp.transpose(b_cat, (0, 2, 1, 3)).reshape(depth, LRU_BLOCKS, 1, 4 * bw)
    lam = jnp.transpose(lru_lambda.reshape(depth, 2, LRU_BLOCKS, bw), (0, 2, 1, 3))
    sink_rows = jnp.repeat(attn_sink.reshape(depth, N_KV_HEADS, GROUP), WINDOW, axis=-1)[..., None]
    sink_rows = jnp.broadcast_to(sink_rows, (depth, N_KV_HEADS, GROUP * WINDOW, HEAD_DIM))

    def run(x, row0):
        b, seq, _ = x.shape
        x2 = x.reshape(b * seq, d)
        for l in range(depth):
            q, k, v, xr, yg = _in_proj(x2, mod[l], row0, seq, norm1_g[l], w_in_b, l, q_norm_g[l], k_norm_g[l])
            sh = lambda a: a.reshape(a.shape[0], b, seq, a.shape[-1])
            flat = lambda a: a.reshape(a.shape[0], b * seq, a.shape[-1])
            attn = _attention(sh(q), sh(k), sh(v), bias, sink_rows[l])
            lru = _lru(sh(xr), sh(yg), lru_conv_w[l], lru_conv_b[l], w_cat[l], b_cat[l], lam[l])
            x2, h2 = _out_proj(x2, flat(attn), flat(lru), mod[l], row0, seq,
                               attn_out_g[l], lru_out_g[l], w_out_b, l, norm2_g[l])
            x2 = _ffn(x2, h2, mod[l], row0, seq, w_gate_b, w_up_b, ffn_conv_w[l], ffn_conv_b[l], w_down_b, l)
        return x2.reshape(b, seq, d)

    return (run(x_prompt, 0), run(x_sample, n_prompt))
```

```python
import functools
import math

import jax
import jax.numpy as jnp
from jax import lax
from jax.experimental import pallas as pl
from jax.experimental.pallas import tpu as pltpu

F32 = jnp.float32
BF16 = jnp.bfloat16

EPS = 1e-6
HEAD_DIM = 128
N_HEADS = 8
N_KV_HEADS = 2
GROUP = N_HEADS // N_KV_HEADS
WINDOW = 128
N_BUCKETS = 32
LRU_BLOCKS = 8
LRU_BLOCK_DIM = 128
LRU_CONV = 4
LRU_C = 8.0
FFN_CONV = 3
NEG_INF = -1e30
LOG2E = math.log2(math.e)
LN2 = math.log(2.0)
SERIES_LIMIT = 0.01
F32_TINY = float(jnp.finfo(jnp.float32).tiny)

SUBLANES = 8
LANES = 128
MIB = 1024 * 1024

TOKEN_TILE = 512
FFN_TOKEN_TILE = 512
FF_CHUNK = 1024
MOD_CHUNK = 1024
ATTN_Q_TILE = 2048
LRU_CHUNK = 2048
LRU_SCAN_UNROLL = 12
LRU_SCAN_BATCH = 6
LRU_LOOKAHEAD = 3
ATTN_UNROLL = 4
HALO = SUBLANES


def _cparams(semantics, vmem_mib):
    return pltpu.CompilerParams(dimension_semantics=semantics, vmem_limit_bytes=vmem_mib * MIB)


def _dot(a, b):
    return jnp.dot(a, b, preferred_element_type=F32)


def _rms(x, gain):
    return x * lax.rsqrt(jnp.mean(x * x, axis=-1, keepdims=True) + EPS) * gain


def _gelu_tanh(x):
    return 0.5 * x * (1.0 + jnp.tanh(math.sqrt(2.0 / math.pi) * (x + 0.044715 * (x * x * x))))


def _sigmoid(x):
    return 1.0 / (1.0 + jnp.exp(-x))


def _one_minus_exp(x, exp_x):
    series = x * (-1.0 + x * (-1 / 2 + x * (-1 / 6)))
    return jnp.where(x > -SERIES_LIMIT, series, 1.0 - exp_x)


def _mod_kernel(c_ref, w_ref, b_ref, o_ref):
    c = c_ref[...]
    cs = (c * _sigmoid(c)).astype(BF16)
    o_ref[0] = _dot(cs, w_ref[0].astype(BF16)) + b_ref[0]


def _modulation(c_all, w_mod, b_mod):
    depth, d, n = w_mod.shape
    rows = c_all.shape[0]
    return pl.pallas_call(
        _mod_kernel,
        grid=(depth, n // MOD_CHUNK),
        in_specs=[
            pl.BlockSpec((rows, d), lambda l, j: (0, 0)),
            pl.BlockSpec((1, d, MOD_CHUNK), lambda l, j: (l, 0, j)),
            pl.BlockSpec((1, 1, MOD_CHUNK), lambda l, j: (l, 0, j)),
        ],
        out_specs=pl.BlockSpec((1, rows, MOD_CHUNK), lambda l, j: (l, 0, j)),
        out_shape=jax.ShapeDtypeStruct((depth, rows, n), F32),
        compiler_params=_cparams(("arbitrary", "arbitrary"), 40),
        name="adaln_modulation",
    )(c_all, w_mod, b_mod.reshape(depth, 1, n))


_BUCKET_THRESHOLDS = (12, 16, 23, 32, 46, 64, 91)


def _bias_kernel(rb_ref, o_ref):
    head = pl.program_id(0)
    qi = lax.broadcasted_iota(jnp.int32, (WINDOW, 3 * WINDOW), 0)
    kj = lax.broadcasted_iota(jnp.int32, (WINDOW, 3 * WINDOW), 1)
    rel = kj - WINDOW - qi
    n = jnp.abs(rel)
    half = N_BUCKETS // 2
    large = jnp.full_like(n, half // 2)
    for t in _BUCKET_THRESHOLDS:
        large = large + jnp.where(n >= t, 1, 0)
    bucket = jnp.where(rel > 0, half, 0) + jnp.where(n < half // 2, n, large)
    bias = jnp.zeros((WINDOW, 3 * WINDOW), F32)
    for b in range(N_BUCKETS):
        bias = jnp.where(bucket == b, rb_ref[b, head], bias)
    o_ref[0] = jnp.where(n <= WINDOW, bias * LOG2E, NEG_INF)


def _bias_table(rel_bias):
    out = pl.pallas_call(
        _bias_kernel,
        grid=(N_HEADS,),
        in_specs=[pl.BlockSpec(memory_space=pltpu.SMEM)],
        out_specs=pl.BlockSpec((1, WINDOW, 3 * WINDOW), lambda h: (h, 0, 0)),
        out_shape=jax.ShapeDtypeStruct((N_HEADS, WINDOW, 3 * WINDOW), F32),
        compiler_params=_cparams(("arbitrary",), 16),
        name="rel_bias_table",
    )(rel_bias)
    return out.reshape(N_KV_HEADS, GROUP * WINDOW, 3 * WINDOW)


def _in_kernel(x_ref, mod_ref, g_ref, w_ref, qg_ref, kg_ref, q_ref, k_ref, v_ref, xr_ref, yg_ref):
    mod = mod_ref[0]
    h = _rms(x_ref[...], g_ref[...] * (1.0 + mod[1:2])) + mod[0:1]
    hb = h.astype(BF16)
    nc = GROUP * HEAD_DIM
    aw = N_KV_HEADS * nc
    kvw = N_KV_HEADS * HEAD_DIM
    lw = LRU_BLOCKS * LRU_BLOCK_DIM
    per = nc // LRU_BLOCK_DIM

    def lru_part(dst_ref, base, act):
        for c in range(LRU_BLOCKS // per):
            z = _dot(hb, w_ref[:, base + c * nc:base + (c + 1) * nc])
            z = z if act is None else act(z)
            for i in range(per):
                dst_ref[c * per + i] = z[:, i * LRU_BLOCK_DIM:(i + 1) * LRU_BLOCK_DIM]

    lru_part(yg_ref, aw + 2 * kvw + lw, _gelu_tanh)
    for c in range(N_KV_HEADS):
        z = _dot(hb, w_ref[:, c * nc:(c + 1) * nc])
        for i in range(GROUP):
            zh = z[:, i * HEAD_DIM:(i + 1) * HEAD_DIM]
            q_ref[c, :, i * HEAD_DIM:(i + 1) * HEAD_DIM] = _rms(zh, qg_ref[...]).astype(BF16)
    z = _dot(hb, w_ref[:, aw:aw + 2 * kvw])
    for i in range(N_KV_HEADS):
        k_ref[i] = _rms(z[:, i * HEAD_DIM:(i + 1) * HEAD_DIM], kg_ref[...]).astype(BF16)
        v_ref[i] = z[:, kvw + i * HEAD_DIM:kvw + (i + 1) * HEAD_DIM].astype(BF16)
    lru_part(xr_ref, aw + 2 * kvw, None)


def _in_proj(x2, mod6, row0, seq, norm_g, w_in, layer, q_g, k_g):
    t, d = x2.shape
    gw = GROUP * HEAD_DIM
    tm = TOKEN_TILE
    tiles_per_seq = seq // tm
    blocked = lambda n, w: pl.BlockSpec((n, tm, w), lambda i: (0, i, 0))
    const = lambda shape: pl.BlockSpec(shape, lambda i: (0,) * len(shape))
    return pl.pallas_call(
        _in_kernel,
        grid=(t // tm,),
        in_specs=[
            pl.BlockSpec((tm, d), lambda i: (i, 0)),
            pl.BlockSpec((1, 6, d), lambda i: (row0 + i // tiles_per_seq, 0, 0)),
            const((1, d)),
            pl.BlockSpec((None,) + w_in.shape[1:], lambda i: (layer, 0, 0)),
            const((1, HEAD_DIM)),
            const((1, HEAD_DIM)),
        ],
        out_specs=[blocked(N_KV_HEADS, gw), blocked(N_KV_HEADS, HEAD_DIM), blocked(N_KV_HEADS, HEAD_DIM),
                   blocked(LRU_BLOCKS, LRU_BLOCK_DIM), blocked(LRU_BLOCKS, LRU_BLOCK_DIM)],
        out_shape=[
            jax.ShapeDtypeStruct((N_KV_HEADS, t, gw), BF16),
            jax.ShapeDtypeStruct((N_KV_HEADS, t, HEAD_DIM), BF16),
            jax.ShapeDtypeStruct((N_KV_HEADS, t, HEAD_DIM), BF16),
            jax.ShapeDtypeStruct((LRU_BLOCKS, t, LRU_BLOCK_DIM), F32),
            jax.ShapeDtypeStruct((LRU_BLOCKS, t, LRU_BLOCK_DIM), F32),
        ],
        compiler_params=_cparams(("parallel",), 52),
        name="in_proj",
    )(x2, mod6, norm_g.reshape(1, d), w_in, q_g.reshape(1, HEAD_DIM), k_g.reshape(1, HEAD_DIM))


def _attn_kernel(q_ref, k_ref, v_ref, bias_ref, sink_ref, o_ref, *, seq):
    nb = seq // WINDOW
    blocks = q_ref.shape[0] // WINDOW
    chunk = pl.program_id(2)
    scale = HEAD_DIM ** -0.5 * LOG2E
    sink = sink_ref[0] * LOG2E
    col = lax.broadcasted_iota(jnp.int32, (1, 3 * WINDOW), 1)
    ones = jnp.ones((3 * WINDOW, HEAD_DIM), BF16)
    rows = GROUP * WINDOW

    def scores(j):
        n = chunk * blocks + j
        r0 = j * WINDOW
        qs = jnp.concatenate(
            [q_ref[pl.ds(r0, WINDOW), g * HEAD_DIM:(g + 1) * HEAD_DIM] for g in range(GROUP)], axis=0)
        starts = [pl.multiple_of(jnp.maximum(n - 1, 0) * WINDOW, WINDOW),
                  pl.multiple_of(n * WINDOW, WINDOW),
                  pl.multiple_of(jnp.minimum(n + 1, nb - 1) * WINDOW, WINDOW)]
        kw = jnp.concatenate([k_ref[pl.ds(s, WINDOW), :] for s in starts], axis=0)
        vw = jnp.concatenate(
            [jnp.concatenate([v_ref[pl.ds(s, WINDOW), :] for s in starts], axis=0), ones], axis=1)
        s = lax.dot_general(qs, kw, (((1,), (1,)), ((), ())), preferred_element_type=F32)
        return n, r0, s, vw

    def softmax(n, r0, s, vw):
        s = s * scale + bias_ref[0]
        valid = ((col >= WINDOW) | (n > 0)) & ((col < 2 * WINDOW) | (n < nb - 1))
        s = jnp.where(valid, s, NEG_INF)
        m = jnp.maximum(jnp.broadcast_to(jnp.max(s, axis=-1, keepdims=True), (rows, HEAD_DIM)), sink)
        p = jnp.exp2(s - jnp.concatenate([m, m, m], axis=1)).astype(BF16)
        return r0, p, jnp.exp2(sink - m), vw

    def finish(r0, p, sink_term, vw):
        ov = _dot(p, vw)
        o = ov[:, :HEAD_DIM] / (ov[:, HEAD_DIM:] + sink_term)
        for g in range(GROUP):
            o_ref[pl.ds(r0, WINDOW), g * HEAD_DIM:(g + 1) * HEAD_DIM] = o[g * WINDOW:(g + 1) * WINDOW]

    groups = [range(g0, g0 + ATTN_UNROLL) for g0 in range(0, blocks, ATTN_UNROLL)]
    stage = [scores(j) for j in groups[0]]
    for g in range(len(groups)):
        ahead = [scores(j) for j in groups[g + 1]] if g + 1 < len(groups) else None
        for st in [softmax(*st) for st in stage]:
            finish(*st)
        stage = ahead


def _attention(q, k, v, bias, sink_rows):
    _, b, seq, gw = q.shape
    tq = min(ATTN_Q_TILE, seq)
    seq_blk = pl.BlockSpec((None, None, seq, HEAD_DIM), lambda bi, h, c: (h, bi, 0, 0))
    return pl.pallas_call(
        functools.partial(_attn_kernel, seq=seq),
        grid=(b, N_KV_HEADS, seq // tq),
        in_specs=[
            pl.BlockSpec((None, None, tq, gw), lambda bi, h, c: (h, bi, c, 0)),
            seq_blk,
            seq_blk,
            pl.BlockSpec((1, GROUP * WINDOW, 3 * WINDOW), lambda bi, h, c: (h, 0, 0)),
            pl.BlockSpec((1, GROUP * WINDOW, HEAD_DIM), lambda bi, h, c: (h, 0, 0)),
        ],
        out_specs=pl.BlockSpec((None, None, tq, gw), lambda bi, h, c: (h, bi, c, 0)),
        out_shape=jax.ShapeDtypeStruct((N_KV_HEADS, b, seq, gw), F32),
        compiler_params=_cparams(("parallel", "parallel", "arbitrary"), 40),
        name="window_attention",
    )(q, k, v, bias, sink_rows)


def _lru_kernel(xr_ref, yg_ref, cw_ref, cb_ref, w_ref, gb_ref, lam_ref, o_ref,
                xp_ref, af_ref, bf_ref, ab_ref, bb_ref, carry_ref, *, seq, pitch):
    tc = min(LRU_CHUNK, seq)
    n_chunks = seq // tc
    pad = SUBLANES
    zeros8 = jnp.zeros((pad, LANES), F32)
    xp_ref[0:pad, :] = zeros8
    xp_ref[seq + pad:seq + 2 * pad, :] = zeros8
    tail = SUBLANES * pitch - seq
    for r in (af_ref, bf_ref, ab_ref, bb_ref):
        r[seq:seq + tail, :] = jnp.zeros((tail, LANES), F32)

    def copy_body(c, carry):
        t0 = pl.multiple_of(c * tc, tc)
        xp_ref[pl.ds(t0 + pad, tc), :] = xr_ref[pl.ds(t0, tc), :]
        return carry

    lax.fori_loop(0, n_chunks, copy_body, 0)

    half_log2_decay = (-0.5 * LRU_C * LOG2E) * jax.nn.softplus(-lam_ref[0])
    cw = 0.5 * cw_ref[...]
    cb = 0.5 * cb_ref[...]
    gb = 0.5 * gb_ref[0]
    bw = LRU_BLOCK_DIM

    def gate_body(c, carry):
        t0 = pl.multiple_of(c * tc, tc)
        xh = xp_ref[pl.ds(t0 + pad - 2, tc), :] * cw[0:1] + cb
        for kk in range(1, LRU_CONV):
            xh = xh + xp_ref[pl.ds(t0 + pad - 2 + kk, tc), :] * cw[kk:kk + 1]
        t = jnp.tanh(_dot(xh.astype(BF16), w_ref[0]) + gb)
        for d, (a_ref, b_ref) in enumerate(((af_ref, bf_ref), (ab_ref, bb_ref))):
            log2_a = (1.0 + t[:, d * bw:(d + 1) * bw]) * half_log2_decay[d:d + 1]
            a = jnp.exp2(log2_a)
            a_ref[pl.ds(t0, tc), :] = a
            y = _one_minus_exp(log2_a * (2.0 * LN2), a * a)
            root = y * lax.rsqrt(jnp.maximum(y, F32_TINY))
            b_ref[pl.ds(t0, tc), :] = root * (1.0 + t[:, (2 + d) * bw:(3 + d) * bw]) * xh
        return carry

    lax.fori_loop(0, n_chunks, gate_body, 0)

    def seg(i):
        return pl.ds(i, SUBLANES, stride=pitch)

    def scan_steps(a_ref, b_ref, steps, h, p):
        ab = [(a_ref[seg(i), :], b_ref[seg(i), :]) for i in steps]
        for g0 in range(0, len(steps), LRU_LOOKAHEAD):
            a_run = b_run = None
            for i, (a, b) in zip(steps[g0:g0 + LRU_LOOKAHEAD], ab[g0:g0 + LRU_LOOKAHEAD]):
                a_run, b_run = (a, b) if a_run is None else (a * a_run, a * b_run + b)
                h_i = a_run * h + b_run
                p_i = a_run * p
                b_ref[seg(i), :] = h_i
                a_ref[seg(i), :] = p_i
            h, p = h_i, p_i
        return h, p

    def scan_body(it, carry):
        hf, pf, hb, pb = carry
        for k0 in range(0, LRU_SCAN_UNROLL, LRU_SCAN_BATCH):
            i0 = it * LRU_SCAN_UNROLL + k0
            hf, pf = scan_steps(af_ref, bf_ref, [i0 + k for k in range(LRU_SCAN_BATCH)], hf, pf)
            hb, pb = scan_steps(ab_ref, bb_ref, [pitch - 1 - i0 - k for k in range(LRU_SCAN_BATCH)], hb, pb)
        return hf, pf, hb, pb

    zero = jnp.zeros((SUBLANES, LANES), F32)
    one = jnp.ones((SUBLANES, LANES), F32)
    hf, pf, hb, pb = lax.fori_loop(0, pitch // LRU_SCAN_UNROLL, scan_body, (zero, one, zero, one))

    row = lax.broadcasted_iota(jnp.int32, (SUBLANES, LANES), 0)
    cf = zero
    c = jnp.zeros((1, LANES), F32)
    for s in range(1, SUBLANES):
        c = pf[s - 1:s] * c + hf[s - 1:s]
        cf = jnp.where(row == s, c, cf)
    cbk = zero
    c = jnp.zeros((1, LANES), F32)
    for s in range(SUBLANES - 2, -1, -1):
        c = pb[s + 1:s + 2] * c + hb[s + 1:s + 2]
        cbk = jnp.where(row == s, c, cbk)
    carry_ref[0:SUBLANES, :] = cf
    carry_ref[SUBLANES:2 * SUBLANES, :] = cbk

    to = min(LRU_CHUNK, seq // SUBLANES)
    rows = lax.broadcasted_iota(jnp.int32, (to, LANES), 0)

    def out_body(c, carry):
        t0 = pl.multiple_of(c * to, to)
        s0 = lax.div(t0, pitch)
        s1 = jnp.minimum(s0 + 1, SUBLANES - 1)
        in_first = rows < (s0 + 1) * pitch - t0
        cf_rows = jnp.where(in_first, carry_ref[pl.ds(s0, 1), :], carry_ref[pl.ds(s1, 1), :])
        cb_rows = jnp.where(in_first, carry_ref[pl.ds(SUBLANES + s0, 1), :],
                            carry_ref[pl.ds(SUBLANES + s1, 1), :])
        sl = pl.ds(t0, to)
        h = bf_ref[sl, :] + af_ref[sl, :] * cf_rows + bb_ref[sl, :] + ab_ref[sl, :] * cb_rows
        o_ref[sl, :] = h * yg_ref[sl, :]
        return carry

    lax.fori_loop(0, seq // to, out_body, 0)


def _lru(xr, yg, conv_w, conv_b, w_cat, b_cat, lam):
    _, b, seq, bw = xr.shape
    lw = LRU_BLOCKS * bw
    pitch = seq // SUBLANES
    while pitch % (2 * LRU_SCAN_UNROLL) != LRU_SCAN_UNROLL:
        pitch += 1
    seq_blk = pl.BlockSpec((None, None, seq, bw), lambda bi, n: (n, bi, 0, 0))
    scan_rows = SUBLANES * pitch
    return pl.pallas_call(
        functools.partial(_lru_kernel, seq=seq, pitch=pitch),
        grid=(b, LRU_BLOCKS),
        in_specs=[
            seq_blk,
            seq_blk,
            pl.BlockSpec((LRU_CONV, bw), lambda bi, n: (0, n)),
            pl.BlockSpec((1, bw), lambda bi, n: (0, n)),
            pl.BlockSpec((1, bw, 4 * bw), lambda bi, n: (n, 0, 0)),
            pl.BlockSpec((1, 1, 4 * bw), lambda bi, n: (n, 0, 0)),
            pl.BlockSpec((1, 2, bw), lambda bi, n: (n, 0, 0)),
        ],
        out_specs=seq_blk,
        out_shape=jax.ShapeDtypeStruct((LRU_BLOCKS, b, seq, bw), F32),
        scratch_shapes=[
            pltpu.VMEM((seq + 2 * SUBLANES, LANES), F32),
            pltpu.VMEM((scan_rows, LANES), F32),
            pltpu.VMEM((scan_rows, LANES), F32),
            pltpu.VMEM((scan_rows, LANES), F32),
            pltpu.VMEM((scan_rows, LANES), F32),
            pltpu.VMEM((2 * SUBLANES, LANES), F32),
        ],
        compiler_params=_cparams(("parallel", "parallel"), 52),
        name="rglru",
    )(xr, yg, conv_w, conv_b.reshape(1, lw), w_cat, b_cat, lam)


def _out_kernel(x_ref, attn_ref, lru_ref, mod_ref, ag_ref, lg_ref, w_ref, n2_ref, o_ref, h_ref):
    attn = jnp.concatenate([attn_ref[c] for c in range(attn_ref.shape[0])], axis=1)
    lru = jnp.concatenate([lru_ref[c] for c in range(lru_ref.shape[0])], axis=1)
    na = _rms(attn, ag_ref[...]).astype(BF16)
    nl = _rms(lru, lg_ref[...]).astype(BF16)
    y = _dot(jnp.concatenate([na, nl], axis=1), w_ref[...])
    mod = mod_ref[0]
    x1 = x_ref[...] + mod[2:3] * y
    o_ref[...] = x1
    h_ref[...] = (_rms(x1, n2_ref[...] * (1.0 + mod[4:5])) + mod[3:4]).astype(BF16)


def _out_proj(x2, attn3, lru3, mod6, row0, seq, attn_g, lru_g, w_out, layer, norm2_g):
    t, d = x2.shape
    aw = attn3.shape[0] * attn3.shape[2]
    lw = lru3.shape[0] * lru3.shape[2]
    tm = TOKEN_TILE
    tiles_per_seq = seq // tm
    tok = lambda w: pl.BlockSpec((tm, w), lambda i: (i, 0))
    blocked = lambda a: pl.BlockSpec((a.shape[0], tm, a.shape[2]), lambda i: (0, i, 0))
    const = lambda shape: pl.BlockSpec(shape, lambda i: (0,) * len(shape))
    return pl.pallas_call(
        _out_kernel,
        grid=(t // tm,),
        in_specs=[
            tok(d), blocked(attn3), blocked(lru3),
            pl.BlockSpec((1, 6, d), lambda i: (row0 + i // tiles_per_seq, 0, 0)),
            const((1, aw)), const((1, lw)),
            pl.BlockSpec((None,) + w_out.shape[1:], lambda i: (layer, 0, 0)),
            const((1, d)),
        ],
        out_specs=[tok(d), tok(d)],
        out_shape=[jax.ShapeDtypeStruct((t, d), F32), jax.ShapeDtypeStruct((t, d), BF16)],
        compiler_params=_cparams(("parallel",), 52),
        name="out_proj",
    )(x2, attn3, lru3, mod6, attn_g.reshape(1, aw), lru_g.reshape(1, lw), w_out, norm2_g.reshape(1, d))


def _halo_kernel(h_ref, w_ref, o_ref):
    o_ref[...] = _dot(h_ref[...], w_ref[...])


def _ffn_halo(h2, seq, tm, w_gate, layer):
    t, d = h2.shape
    f = w_gate.shape[2]
    n_tiles = t // tm
    tiles_per_seq = seq // tm
    tiles = h2.reshape(n_tiles, tm, d)
    zero_row = jnp.zeros((1, d), h2.dtype)
    pos = jnp.arange(n_tiles) % tiles_per_seq
    prev = jnp.concatenate([zero_row, tiles[:-1, tm - 1]], axis=0)
    prev = jnp.where((pos == 0)[:, None], 0, prev)
    nxt = jnp.concatenate([tiles[1:, 0], zero_row], axis=0)
    nxt = jnp.where((pos == tiles_per_seq - 1)[:, None], 0, nxt)
    rows = jnp.stack([prev, nxt], axis=1).reshape(2 * n_tiles, d)
    out = pl.pallas_call(
        _halo_kernel,
        grid=(f // FF_CHUNK,),
        in_specs=[
            pl.BlockSpec((2 * n_tiles, d), lambda j: (0, 0)),
            pl.BlockSpec((None, d, FF_CHUNK), lambda j: (layer, 0, j)),
        ],
        out_specs=pl.BlockSpec((2 * n_tiles, FF_CHUNK), lambda j: (0, j)),
        out_shape=jax.ShapeDtypeStruct((2 * n_tiles, f), F32),
        compiler_params=_cparams(("arbitrary",), 32),
        name="ffn_halo",
    )(rows, w_gate)
    return out.reshape(n_tiles, 2, f)


def _ffn_kernel(x_ref, h_ref, gh_ref, mod_ref, wg_ref, wu_ref, cw_ref, cb_ref, wd_ref, o_ref, ge_ref):
    j = pl.program_id(1)
    tm = x_ref.shape[0]
    fc = wg_ref.shape[1]

    @pl.when(j == 0)
    def _():
        o_ref[...] = x_ref[...]

    cols = pl.ds(pl.multiple_of(j * fc, fc), fc)
    h = h_ref[...]
    ge_ref[HALO:HALO + tm, :] = _dot(h, wg_ref[...])
    ge_ref[HALO - 1:HALO, :] = gh_ref[0, 0:1, cols]
    ge_ref[HALO + tm:HALO + tm + 1, :] = gh_ref[0, 1:2, cols]
    u = _dot(h, wu_ref[...])
    cw = cw_ref[:, cols]
    g = ge_ref[HALO - 1:HALO - 1 + tm, :] * cw[0:1] + cb_ref[:, cols]
    g = g + ge_ref[HALO:HALO + tm, :] * cw[1:2]
    g = g + ge_ref[HALO + 1:HALO + 1 + tm, :] * cw[2:3]
    act = (_gelu_tanh(g) * u).astype(BF16)
    o_ref[...] += mod_ref[0][5:6] * _dot(act, wd_ref[...])


def _ffn(x2, h2, mod6, row0, seq, w_gate, w_up, conv_w, conv_b, w_down, layer):
    t, d = x2.shape
    f = w_gate.shape[2]
    fc = FF_CHUNK
    tm = min(FFN_TOKEN_TILE, seq)
    tiles_per_seq = seq // tm
    ge_halo = _ffn_halo(h2, seq, tm, w_gate, layer)
    return pl.pallas_call(
        _ffn_kernel,
        grid=(t // tm, f // fc),
        in_specs=[
            pl.BlockSpec((tm, d), lambda i, j: (i, 0)),
            pl.BlockSpec((tm, d), lambda i, j: (i, 0)),
            pl.BlockSpec((1, 2, f), lambda i, j: (i, 0, 0)),
            pl.BlockSpec((1, 6, d), lambda i, j: (row0 + i // tiles_per_seq, 0, 0)),
            pl.BlockSpec((None, d, fc), lambda i, j: (layer, 0, j)),
            pl.BlockSpec((None, d, fc), lambda i, j: (layer, 0, j)),
            pl.BlockSpec((FFN_CONV, f), lambda i, j: (0, 0)),
            pl.BlockSpec((1, f), lambda i, j: (0, 0)),
            pl.BlockSpec((None, fc, d), lambda i, j: (layer, j, 0)),
        ],
        out_specs=pl.BlockSpec((tm, d), lambda i, j: (i, 0)),
        out_shape=jax.ShapeDtypeStruct((t, d), F32),
        scratch_shapes=[pltpu.VMEM((tm + 2 * HALO, fc), F32)],
        compiler_params=_cparams(("parallel", "arbitrary"), 57),
        name="convglu_ffn",
    )(x2, h2, ge_halo, mod6, w_gate, w_up, conv_w, conv_b.reshape(1, f), w_down)


def kernel(x_prompt, x_sample, c_prompt, c_sample, rel_bias, w_mod, b_mod, norm1_g, norm2_g, w_in, q_norm_g, k_norm_g, attn_sink, lru_conv_w, lru_conv_b, lru_w_a, lru_b_a, lru_w_x, lru_b_x, lru_lambda, attn_out_g, lru_out_g, w_out, ffn_w_gate, ffn_w_up, ffn_conv_w, ffn_conv_b, ffn_w_down):
    depth, d, _ = w_in.shape
    n_prompt = c_prompt.shape[0]
    n_req = n_prompt + c_sample.shape[0]
    rows = -(-n_req // SUBLANES) * SUBLANES
    c_all = jnp.concatenate([c_prompt, c_sample, jnp.zeros((rows - n_req, d), F32)], axis=0)
    mod = _modulation(c_all, w_mod, b_mod).reshape(depth, rows, 6, d)
    bias = _bias_table(rel_bias)

    w_in_b = w_in.astype(BF16)
    w_out_b = w_out.astype(BF16)
    w_gate_b = ffn_w_gate.astype(BF16)
    w_up_b = ffn_w_up.astype(BF16)
    w_down_b = ffn_w_down.astype(BF16)
    w_cat = jnp.concatenate([lru_w_a[:, 0], lru_w_a[:, 1], lru_w_x[:, 0], lru_w_x[:, 1]], axis=-1).astype(BF16)
    bw = LRU_BLOCK_DIM
    b_cat = jnp.concatenate(
        [lru_b_a.reshape(depth, 2, LRU_BLOCKS, bw), lru_b_x.reshape(depth, 2, LRU_BLOCKS, bw)], axis=1)
    b_cat = jnp.transpose(b_cat, (0, 2, 1, 3)).reshape(depth, LRU_BLOCKS, 1, 4 * bw)
    lam = jnp.transpose(lru_lambda.reshape(depth, 2, LRU_BLOCKS, bw), (0, 2, 1, 3))
    sink_rows = jnp.repeat(attn_sink.reshape(depth, N_KV_HEADS, GROUP), WINDOW, axis=-1)[..., None]
    sink_rows = jnp.broadcast_to(sink_rows, (depth, N_KV_HEADS, GROUP * WINDOW, HEAD_DIM))

    def run(x, row0):
        b, seq, _ = x.shape
        x2 = x.reshape(b * seq, d)
        for l in range(depth):
            q, k, v, xr, yg = _in_proj(x2, mod[l], row0, seq, norm1_g[l], w_in_b, l, q_norm_g[l], k_norm_g[l])
            sh = lambda a: a.reshape(a.shape[0], b, seq, a.shape[-1])
            flat = lambda a: a.reshape(a.shape[0], b * seq, a.shape[-1])
            attn = _attention(sh(q), sh(k), sh(v), bias, sink_rows[l])
            lru = _lru(sh(xr), sh(yg), lru_conv_w[l], lru_conv_b[l], w_cat[l], b_cat[l], lam[l])
            x2, h2 = _out_proj(x2, flat(attn), flat(lru), mod[l], row0, seq,
                               attn_out_g[l], lru_out_g[l], w_out_b, l, norm2_g[l])
            x2 = _ffn(x2, h2, mod[l], row0, seq, w_gate_b, w_up_b, ffn_conv_w[l], ffn_conv_b[l], w_down_b, l)
        return x2.reshape(b, seq, d)

    return (run(x_prompt, 0), run(x_sample, n_prompt))
```
